```python
import jax, jax.numpy as jnp
from jax import lax
import numpy as np

D_MODEL = 2048
BATCH = 4
SEQ = 4096
DEPTH = 4

D_CONV = D_MODEL // 4
CONV_WIDTH = 31
N_HEADS = 16
HEAD_DIM = 64
D_ATTN = N_HEADS * HEAD_DIM
D_POOL = D_MODEL // 4
POOL_WINDOWS = (2, 4, 8, 16)
N_POOL_GROUPS = 4
POOL_GROUP = D_POOL // N_POOL_GROUPS
N_BRANCH = 3
IN_COLS = 2 * D_CONV + 3 * D_ATTN + D_POOL + N_BRANCH * D_MODEL
D_FF = ((8 * D_MODEL // 3 + 127) // 128) * 128
N_ADA = 9
Q_BLOCK = 128
EPS = 1e-6

kernel_name = "hybrid_gated_conv_stickbreak_pool_macaron"


def rmsnorm(x, g):
    xf = x.astype(jnp.float32)
    y = xf * lax.rsqrt(jnp.mean(xf * xf, axis=-1, keepdims=True) + EPS)
    return (y * g.astype(jnp.float32)).astype(x.dtype)


def layernorm(x, g, b):
    xf = x.astype(jnp.float32)
    mu = jnp.mean(xf, axis=-1, keepdims=True)
    var = jnp.mean(jnp.square(xf - mu), axis=-1, keepdims=True)
    y = (xf - mu) * lax.rsqrt(var + EPS)
    return (y * g.astype(jnp.float32) + b.astype(jnp.float32)).astype(x.dtype)


def modulate(x, g, shift, scale):
    return rmsnorm(x, g) * (1 + scale[:, None, :]) + shift[:, None, :]


def swiglu(h, w_gu, w_d):
    gate, up = jnp.split(h @ w_gu, 2, axis=-1)
    return (jax.nn.silu(gate) * up) @ w_d


def stick_breaking_attention(q, k, v):
    B, S, H, dh = q.shape
    nb = S // Q_BLOCK
    qb = q.reshape(B, nb, Q_BLOCK, H, dh).transpose(1, 0, 3, 2, 4)
    kt = k.transpose(0, 2, 1, 3).astype(jnp.float32)
    vt = v.transpose(0, 2, 1, 3)
    key_pos = jnp.arange(S)
    inv_sqrt_d = dh ** -0.5

    def one_block(args):
        q_blk, i = args
        z = jnp.einsum('bhqd,bhkd->bhqk', q_blk.astype(jnp.float32), kt) * inv_sqrt_d
        q_pos = i * Q_BLOCK + jnp.arange(Q_BLOCK)
        mask = key_pos[None, :] < q_pos[:, None]
        log_not = jnp.where(mask, jax.nn.log_sigmoid(-z), 0.0)
        after = lax.cumsum(log_not, axis=3, reverse=True) - log_not
        w = jnp.where(mask, jnp.exp(jax.nn.log_sigmoid(z) + after), 0.0)
        return jnp.einsum('bhqk,bhkd->bhqd', w.astype(vt.dtype), vt)

    out = lax.map(one_block, (qb, jnp.arange(nb)))
    return out.transpose(1, 0, 3, 2, 4).reshape(B, S, H * dh)


def causal_mean(xg, window):
    S = xg.shape[1]
    xf = xg.astype(jnp.float32)
    cs = jnp.cumsum(xf, axis=1)
    cs_shift = jnp.pad(cs, ((0, 0), (window, 0), (0, 0)))[:, :S]
    count = jnp.minimum(jnp.arange(S) + 1, window).astype(jnp.float32)
    return ((cs - cs_shift) / count[None, :, None]).astype(xg.dtype)


def token_mixer(u, w_in, conv_w, conv_b, conv_ln_g, conv_ln_b, pool_w, pool_scale,
                w_br_conv, w_br_attn, w_br_pool, w_out):
    B, S, _ = u.shape
    proj = u @ w_in
    o1 = 2 * D_CONV
    o2 = o1 + 3 * D_ATTN
    o3 = o2 + D_POOL
    conv_in, attn_in, pool_in, gate_logits = proj[..., :o1], proj[..., o1:o2], proj[..., o2:o3], proj[..., o3:]

    a, g = jnp.split(conv_in, 2, axis=-1)
    glu = a * jax.nn.sigmoid(g)
    dw = lax.conv_general_dilated(glu, conv_w[:, None, :], window_strides=(1,),
                                  padding=[(CONV_WIDTH - 1, 0)],
                                  dimension_numbers=('NWC', 'WIO', 'NWC'),
                                  feature_group_count=D_CONV) + conv_b
    y_conv = jax.nn.silu(layernorm(dw, conv_ln_g, conv_ln_b)) @ w_br_conv

    q, k, v = jnp.split(attn_in, 3, axis=-1)
    q = q.reshape(B, S, N_HEADS, HEAD_DIM)
    k = k.reshape(B, S, N_HEADS, HEAD_DIM)
    v = v.reshape(B, S, N_HEADS, HEAD_DIM)
    y_attn = stick_breaking_attention(q, k, v) @ w_br_attn

    groups = []
    for gi, win in enumerate(POOL_WINDOWS):
        pg = pool_in[..., gi * POOL_GROUP:(gi + 1) * POOL_GROUP]
        groups.append(jnp.einsum('bsc,cd->bsd', causal_mean(pg, win) - pg, pool_w[gi]))
    y_pool = (jnp.concatenate(groups, axis=-1) * pool_scale) @ w_br_pool

    gates = jax.nn.sigmoid(gate_logits).reshape(B, S, N_BRANCH, D_MODEL)
    merged = gates[:, :, 0] * y_conv + gates[:, :, 1] * y_attn + gates[:, :, 2] * y_pool
    return merged @ w_out


def setup_inputs(seed: int = 0) -> dict:
    key = jax.random.key(seed)
    ks = jax.random.split(key, 24)
    f32 = jnp.float32

    def nrm(k, shape, fan_in, mult=1.0):
        return jax.random.normal(k, shape, f32) * (mult * fan_in ** -0.5)

    return {
        "x": jax.random.normal(ks[0], (BATCH, SEQ, D_MODEL), f32),
        "c": jax.random.normal(ks[1], (BATCH, D_MODEL), f32),
        "norm_g": 1.0 + 0.1 * jax.random.normal(ks[2], (DEPTH, 3, D_MODEL), f32),
        "w_ada": nrm(ks[3], (DEPTH, D_MODEL, N_ADA * D_MODEL), D_MODEL, 0.5),
        "b_ada": 0.02 * jax.random.normal(ks[4], (DEPTH, N_ADA * D_MODEL), f32),
        "ffn1_w_gu": nrm(ks[5], (DEPTH, D_MODEL, 2 * D_FF), D_MODEL),
        "ffn1_w_d": nrm(ks[6], (DEPTH, D_FF, D_MODEL), D_FF),
        "w_in": nrm(ks[7], (DEPTH, D_MODEL, IN_COLS), D_MODEL),
        "conv_w": nrm(ks[8], (DEPTH, CONV_WIDTH, D_CONV), CONV_WIDTH),
        "conv_b": 0.02 * jax.random.normal(ks[9], (DEPTH, D_CONV), f32),
        "conv_ln_g": 1.0 + 0.1 * jax.random.normal(ks[10], (DEPTH, D_CONV), f32),
        "conv_ln_b": 0.02 * jax.random.normal(ks[11], (DEPTH, D_CONV), f32),
        "pool_w": nrm(ks[12], (DEPTH, N_POOL_GROUPS, POOL_GROUP, POOL_GROUP), POOL_GROUP),
        "pool_scale": 1.0 + 0.1 * jax.random.normal(ks[13], (DEPTH, D_POOL), f32),
        "w_br_conv": nrm(ks[14], (DEPTH, D_CONV, D_MODEL), D_CONV),
        "w_br_attn": nrm(ks[15], (DEPTH, D_ATTN, D_MODEL), D_ATTN),
        "w_br_pool": nrm(ks[16], (DEPTH, D_POOL, D_MODEL), D_POOL),
        "w_out": nrm(ks[17], (DEPTH, D_MODEL, D_MODEL), D_MODEL),
        "ffn2_w_gu": nrm(ks[18], (DEPTH, D_MODEL, 2 * D_FF), D_MODEL),
        "ffn2_w_d": nrm(ks[19], (DEPTH, D_FF, D_MODEL), D_FF),
        "final_g": 1.0 + 0.1 * jax.random.normal(ks[20], (D_MODEL,), f32),
    }


def reference(x, c, norm_g, w_ada, b_ada, ffn1_w_gu, ffn1_w_d, w_in, conv_w, conv_b, conv_ln_g, conv_ln_b,
              pool_w, pool_scale, w_br_conv, w_br_attn, w_br_pool, w_out, ffn2_w_gu, ffn2_w_d, final_g):
    c_act = jax.nn.silu(c)
    h = x
    for l in range(DEPTH):
        ada = c_act @ w_ada[l] + b_ada[l]
        sh1, sc1, g1, sh2, sc2, g2, sh3, sc3, g3 = jnp.split(ada, N_ADA, axis=-1)
        u = modulate(h, norm_g[l, 0], sh1, sc1)
        h = h + 0.5 * g1[:, None, :] * swiglu(u, ffn1_w_gu[l], ffn1_w_d[l])
        u = modulate(h, norm_g[l, 1], sh2, sc2)
        mix = token_mixer(u, w_in[l], conv_w[l], conv_b[l], conv_ln_g[l], conv_ln_b[l], pool_w[l], pool_scale[l],
                          w_br_conv[l], w_br_attn[l], w_br_pool[l], w_out[l])
        h = h + g2[:, None, :] * mix
        u = modulate(h, norm_g[l, 2], sh3, sc3)
        h = h + 0.5 * g3[:, None, :] * swiglu(u, ffn2_w_gu[l], ffn2_w_d[l])
    return rmsnorm(h, final_g)
```

```python
import functools

import jax
import jax.numpy as jnp
from jax import lax
from jax.experimental import pallas as pl
from jax.experimental.pallas import tpu as pltpu

F32 = jnp.float32
BF16 = jnp.bfloat16

NORM_EPS = 1e-6
HEAD_DIM = 64
POOL_WINDOWS = (2, 4, 8, 16)
N_ADA_CHUNKS = 9
LANES = 128
V7X_VMEM_LIMIT = 60000 * 1024
CONV_HALO = 32
POOL_HALO = 16


def _sigmoid(x):
    return 1.0 / (1.0 + jnp.exp(-x))


def _cparams(*sem):
    return pltpu.CompilerParams(dimension_semantics=sem, vmem_limit_bytes=V7X_VMEM_LIMIT)


def _modulate(h, g, mod):
    ms = jnp.mean(h * h, axis=-1, keepdims=True)
    y = h * lax.rsqrt(ms + NORM_EPS) * g
    return y * (1.0 + mod[1:2, :]) + mod[0:1, :]


def _ada_kernel(c_ref, w_ref, b_ref, o_ref):
    c = c_ref[...]
    act = (c * _sigmoid(c)).astype(BF16)
    o_ref[0] = jnp.dot(act, w_ref[0].astype(BF16), preferred_element_type=F32) + b_ref[0]


def _ada_all_layers(c_pad, w_ada, b_ada, tn=1024):
    depth, d, n = w_ada.shape
    rows = c_pad.shape[0]
    return pl.pallas_call(
        _ada_kernel,
        grid=(depth, n // tn),
        in_specs=[
            pl.BlockSpec((rows, d), lambda l, j: (0, 0)),
            pl.BlockSpec((1, d, tn), lambda l, j: (l, 0, j)),
            pl.BlockSpec((1, 1, tn), lambda l, j: (l, 0, j)),
        ],
        out_specs=pl.BlockSpec((1, rows, tn), lambda l, j: (l, 0, j)),
        out_shape=jax.ShapeDtypeStruct((depth, rows, n), F32),
        compiler_params=_cparams("arbitrary", "arbitrary"),
        name="ada",
    )(c_pad, w_ada, b_ada.reshape(depth, 1, n))


def _ffn_kernel(h_ref, mod_ref, g_ref, wgu_ref, wd_ref, *rest, tf, nk, final):
    if final:
        fg_ref, o_ref, u_scr = rest
    else:
        o_ref, u_scr = rest
    k = pl.program_id(1)

    @pl.when(k == 0)
    def _():
        u_scr[...] = _modulate(h_ref[...], g_ref[0], mod_ref[0, 0]).astype(BF16)
        o_ref[...] = jnp.zeros_like(o_ref)

    gu = jnp.dot(u_scr[...], wgu_ref[0], preferred_element_type=F32)
    gate = gu[:, :tf]
    act = (gate * _sigmoid(gate) * gu[:, tf:]).astype(BF16)
    o_ref[...] += jnp.dot(act, wd_ref[0], preferred_element_type=F32)

    @pl.when(k == nk - 1)
    def _():
        gate_row = mod_ref[0, 0][2:3, :]
        h_new = h_ref[...] + (0.5 * gate_row) * o_ref[...]
        if final:
            ms = jnp.mean(h_new * h_new, axis=-1, keepdims=True)
            h_new = h_new * lax.rsqrt(ms + NORM_EPS) * fg_ref[...]
        o_ref[...] = h_new


def _ffn(h, ada, norm_g, wgu, wd, layer, sub, seq, tf, final_g=None, tm=512):
    t, d = h.shape
    tm = min(tm, seq)
    nk = wd.shape[1] // tf
    tiles_per_seq = seq // tm
    final = final_g is not None
    in_specs = [
        pl.BlockSpec((tm, d), lambda i, k: (i, 0)),
        pl.BlockSpec((1, 1, 3, d), lambda i, k: (i // tiles_per_seq, sub, 0, 0)),
        pl.BlockSpec((1, 1, d), lambda i, k: (layer * 3 + sub, 0, 0)),
        pl.BlockSpec((1, d, 2 * tf), lambda i, k: (layer, 0, k)),
        pl.BlockSpec((1, tf, d), lambda i, k: (layer, k, 0)),
    ]
    args = [h, ada, norm_g, wgu, wd]
    if final:
        in_specs.append(pl.BlockSpec((1, d), lambda i, k: (0, 0)))
        args.append(final_g)
    return pl.pallas_call(
        functools.partial(_ffn_kernel, tf=tf, nk=nk, final=final),
        grid=(t // tm, nk),
        in_specs=in_specs,
        out_specs=pl.BlockSpec((tm, d), lambda i, k: (i, 0)),
        out_shape=jax.ShapeDtypeStruct((t, d), F32),
        scratch_shapes=[pltpu.VMEM((tm, d), BF16)],
        compiler_params=_cparams("arbitrary", "arbitrary"),
        name="ffn",
    )(*args)


def _proj_kernel(h_ref, mod_ref, g_ref, w_ref, conv_ref, qkv_ref, pool_ref, gate_ref, u_scr, *, bounds, q_tiles):
    j = pl.program_id(1)
    conv_end, qkv_end, pool_end = bounds

    @pl.when(j == 0)
    def _():
        u_scr[...] = _modulate(h_ref[...], g_ref[0], mod_ref[0, 0]).astype(BF16)

    res = jnp.dot(u_scr[...], w_ref[0], preferred_element_type=F32)

    @pl.when(j < conv_end)
    def _():
        conv_ref[...] = res

    @pl.when(jnp.logical_and(j >= conv_end, j < conv_end + q_tiles))
    def _():
        qkv_ref[...] = (res * (HEAD_DIM ** -0.5)).astype(BF16)

    @pl.when(jnp.logical_and(j >= conv_end + q_tiles, j < qkv_end))
    def _():
        qkv_ref[...] = res.astype(BF16)

    @pl.when(jnp.logical_and(j >= qkv_end, j < pool_end))
    def _():
        pool_ref[...] = res

    @pl.when(j >= pool_end)
    def _():
        gate_ref[...] = _sigmoid(res).astype(BF16)


def _proj(h, ada, norm_g, w_in, layer, seq, d_conv2, d_attn3, d_pool, tm=1024, tn=512):
    t, d = h.shape
    n = w_in.shape[2]
    d_gate = n - d_conv2 - d_attn3 - d_pool
    conv_end = d_conv2 // tn
    qkv_end = conv_end + d_attn3 // tn
    pool_end = qkv_end + d_pool // tn
    q_tiles = d_attn3 // 3 // tn
    tm = min(tm, seq)
    tiles_per_seq = seq // tm

    def clamp(j, lo, count):
        return jnp.clip(j - lo, 0, count - 1)

    return pl.pallas_call(
        functools.partial(_proj_kernel, bounds=(conv_end, qkv_end, pool_end), q_tiles=q_tiles),
        grid=(t // tm, n // tn),
        in_specs=[
            pl.BlockSpec((tm, d), lambda i, j: (i, 0)),
            pl.BlockSpec((1, 1, 3, d), lambda i, j: (i // tiles_per_seq, 1, 0, 0)),
            pl.BlockSpec((1, 1, d), lambda i, j: (layer * 3 + 1, 0, 0)),
            pl.BlockSpec((1, d, tn), lambda i, j: (layer, 0, j)),
        ],
        out_specs=[
            pl.BlockSpec((tm, tn), lambda i, j: (i, clamp(j, 0, conv_end))),
            pl.BlockSpec((tm, tn), lambda i, j: (i, clamp(j, conv_end, qkv_end - conv_end))),
            pl.BlockSpec((tm, tn), lambda i, j: (i, clamp(j, qkv_end, pool_end - qkv_end))),
            pl.BlockSpec((tm, tn), lambda i, j: (i, clamp(j, pool_end, d_gate // tn))),
        ],
        out_shape=[
            jax.ShapeDtypeStruct((t, d_conv2), F32),
            jax.ShapeDtypeStruct((t, d_attn3), BF16),
            jax.ShapeDtypeStruct((t, d_pool), F32),
            jax.ShapeDtypeStruct((t, d_gate), BF16),
        ],
        scratch_shapes=[pltpu.VMEM((tm, d), BF16)],
        compiler_params=_cparams("arbitrary", "arbitrary"),
        name="proj",
    )(h, ada, norm_g, w_in)


def _attn_kernel(q_ref, k_ref, v_ref, u_ref, o_ref, acc_ref, c_ref, *, blk, skip_below):
    qi = pl.program_id(2)
    lane = lax.broadcasted_iota(jnp.int32, (blk, LANES), 1)
    lower_head = lane < HEAD_DIM
    q = q_ref[...]
    zero = jnp.zeros_like(q)
    q_heads = (jnp.where(lower_head, q, zero), jnp.where(lower_head, zero, q))
    row = lax.broadcasted_iota(jnp.int32, (blk, blk), 0)
    col = lax.broadcasted_iota(jnp.int32, (blk, blk), 1)
    causal = col < row
    acc_ref[...] = jnp.zeros_like(acc_ref)
    c_ref[...] = jnp.zeros_like(c_ref)

    def tile(kb, diagonal):
        start = pl.multiple_of(kb * blk, blk)
        k = k_ref[pl.ds(start, blk), :]
        v = v_ref[pl.ds(start, blk), :]
        for hd in range(2):
            z = lax.dot_general(q_heads[hd], k, (((1,), (1,)), ((), ())), preferred_element_type=F32)
            sp = jnp.maximum(z, 0.0) + jnp.log(1.0 + jnp.exp(-jnp.abs(z)))
            if diagonal:
                sp = jnp.where(causal, sp, 0.0)
            hi = pltpu.bitcast(pltpu.bitcast(sp, jnp.uint32) & jnp.uint32(0xFFFF0000), F32)
            lo = sp - hi
            parts = jnp.concatenate([hi.astype(BF16), lo.astype(BF16)], axis=1)
            cum = jnp.dot(parts, u_ref[...], preferred_element_type=F32)
            w = jnp.exp(z + cum)
            if diagonal:
                w = jnp.where(causal, w, 0.0)
            pv = jnp.dot(w.astype(BF16), v, preferred_element_type=F32)
            c = c_ref[hd]
            acc_ref[hd] += pv * jnp.exp(c)
            c_ref[hd] = c + jnp.broadcast_to(cum[:, 0:1], (blk, LANES))

    tile(qi, True)

    if skip_below is None:
        def body(n, carry):
            tile(qi - 1 - n, False)
            return carry
        lax.fori_loop(0, qi, body, 0)
    else:
        def live():
            return jnp.maximum(jnp.max(c_ref[0]), jnp.max(c_ref[1])) > skip_below

        def cond(carry):
            kb, go = carry
            return jnp.logical_and(kb >= 0, go)

        def body(carry):
            kb, _ = carry
            tile(kb, False)
            return kb - 1, live()
        lax.while_loop(cond, body, (qi - 1, live()))

    o_ref[...] = jnp.where(lower_head, acc_ref[0], acc_ref[1]).astype(o_ref.dtype)


def _attention(qkv, cum_mat, batch, seq, d_attn, blk, skip_below):
    t = qkv.shape[0]
    nq = seq // blk
    pairs = d_attn // LANES
    return pl.pallas_call(
        functools.partial(_attn_kernel, blk=blk, skip_below=skip_below),
        grid=(batch, pairs, nq),
        in_specs=[
            pl.BlockSpec((blk, LANES), lambda b, p, i: (b * nq + i, p)),
            pl.BlockSpec((seq, LANES), lambda b, p, i: (b, pairs + p)),
            pl.BlockSpec((seq, LANES), lambda b, p, i: (b, 2 * pairs + p)),
            pl.BlockSpec((2 * blk, blk), lambda b, p, i: (0, 0)),
        ],
        out_specs=pl.BlockSpec((blk, LANES), lambda b, p, i: (b * nq + i, p)),
        out_shape=jax.ShapeDtypeStruct((t, d_attn), BF16),
        scratch_shapes=[pltpu.VMEM((2, blk, LANES), F32), pltpu.VMEM((2, blk, LANES), F32)],
        compiler_params=_cparams("arbitrary", "arbitrary", "arbitrary"),
        name="attn",
    )(qkv, qkv, qkv, cum_mat)


def _branch_kernel(ci_ref, ch_ref, pi_ref, ph_ref, at_ref, gt_ref, cw_ref, cv_ref, pw_ref, ps_ref,
                   wbc_ref, wba_ref, wbp_ref, o_ref, glu_scr, sw_scr, pin_scr, pd_scr, *, tm, tiles_per_seq, chunk):
    i = pl.program_id(0)
    tile_in_seq = i % tiles_per_seq
    first = tile_in_seq == 0
    dc = ci_ref.shape[1] // 2
    d = o_ref.shape[1]
    width = cw_ref.shape[1]

    halo = ch_ref[:, :dc] * _sigmoid(ch_ref[:, dc:])
    glu_scr[0:CONV_HALO, :] = jnp.where(first, 0.0, halo)
    glu_scr[CONV_HALO:, :] = ci_ref[:, :dc] * _sigmoid(ci_ref[:, dc:])
    cvec = cv_ref[0]
    for r0 in range(0, tm, chunk):
        acc = jnp.zeros((chunk, dc), F32)
        for tap in range(width):
            off = CONV_HALO - (width - 1) + tap + r0
            acc = acc + glu_scr[off:off + chunk, :] * cw_ref[0, tap:tap + 1, :]
        dw = acc + cvec[0:1, :]
        mu = jnp.mean(dw, axis=-1, keepdims=True)
        cen = dw - mu
        var = jnp.mean(cen * cen, axis=-1, keepdims=True)
        y = cen * lax.rsqrt(var + NORM_EPS) * cvec[1:2, :] + cvec[2:3, :]
        sw_scr[r0:r0 + chunk, :] = (y * _sigmoid(y)).astype(BF16)
    y_conv = jnp.dot(sw_scr[...], wbc_ref[0], preferred_element_type=F32)
    merged = gt_ref[:, 0:d].astype(F32) * y_conv

    y_attn = jnp.dot(at_ref[...], wba_ref[0], preferred_element_type=F32)
    merged = merged + gt_ref[:, d:2 * d].astype(F32) * y_attn

    pin_scr[0:POOL_HALO, :] = jnp.where(first, 0.0, ph_ref[...])
    pin_scr[POOL_HALO:, :] = pi_ref[...]
    pos = tile_in_seq * tm + lax.broadcasted_iota(jnp.int32, (tm, 1), 0)
    group = pw_ref.shape[2]
    for gi, window in enumerate(POOL_WINDOWS):
        lo = gi * group
        total = pin_scr[POOL_HALO:POOL_HALO + tm, lo:lo + group]
        cur = total
        for back in range(1, window):
            total = total + pin_scr[POOL_HALO - back:POOL_HALO - back + tm, lo:lo + group]
        count = jnp.minimum(pos + 1, window).astype(F32)
        diff = total / count - cur
        mapped = jnp.dot(diff.astype(BF16), pw_ref[0, gi], preferred_element_type=F32)
        pd_scr[:, lo:lo + group] = (mapped * ps_ref[0, :, lo:lo + group]).astype(BF16)
    y_pool = jnp.dot(pd_scr[...], wbp_ref[0], preferred_element_type=F32)
    merged = merged + gt_ref[:, 2 * d:3 * d].astype(F32) * y_pool

    o_ref[...] = merged.astype(o_ref.dtype)


def _branches(conv_in, pool_in, attn, gates, conv_w, conv_vec, pool_w, pool_scale, w_br_conv, w_br_attn, w_br_pool,
              layer, seq, d_model, tm=256, chunk=32):
    t, dc2 = conv_in.shape
    dc = dc2 // 2
    dp = pool_in.shape[1]
    da = attn.shape[1]
    width = conv_w.shape[1]
    assert width - 1 <= CONV_HALO and max(POOL_WINDOWS) - 1 <= POOL_HALO
    assert tm % CONV_HALO == 0 and tm % chunk == 0 and seq % tm == 0
    tiles_per_seq = seq // tm
    ngroups = len(POOL_WINDOWS)
    group = dp // ngroups

    def prev_rows(halo):
        return lambda i: (jnp.maximum(i * (tm // halo) - 1, 0), 0)

    const3 = lambda i: (layer, 0, 0)
    return pl.pallas_call(
        functools.partial(_branch_kernel, tm=tm, tiles_per_seq=tiles_per_seq, chunk=chunk),
        grid=(t // tm,),
        in_specs=[
            pl.BlockSpec((tm, dc2), lambda i: (i, 0)),
            pl.BlockSpec((CONV_HALO, dc2), prev_rows(CONV_HALO)),
            pl.BlockSpec((tm, dp), lambda i: (i, 0)),
            pl.BlockSpec((POOL_HALO, dp), prev_rows(POOL_HALO)),
            pl.BlockSpec((tm, da), lambda i: (i, 0)),
            pl.BlockSpec((tm, 3 * d_model), lambda i: (i, 0)),
            pl.BlockSpec((1, width, dc), const3),
            pl.BlockSpec((1, 3, dc), const3),
            pl.BlockSpec((1, ngroups, group, group), lambda i: (layer, 0, 0, 0)),
            pl.BlockSpec((1, 1, dp), const3),
            pl.BlockSpec((1, dc, d_model), const3),
            pl.BlockSpec((1, da, d_model), const3),
            pl.BlockSpec((1, dp, d_model), const3),
        ],
        out_specs=pl.BlockSpec((tm, d_model), lambda i: (i, 0)),
        out_shape=jax.ShapeDtypeStruct((t, d_model), BF16),
        scratch_shapes=[
            pltpu.VMEM((CONV_HALO + tm, dc), F32),
            pltpu.VMEM((tm, dc), BF16),
            pltpu.VMEM((POOL_HALO + tm, dp), F32),
            pltpu.VMEM((tm, dp), BF16),
        ],
        compiler_params=_cparams("arbitrary"),
        name="branches",
    )(conv_in, conv_in, pool_in, pool_in, attn, gates, conv_w, conv_vec, pool_w, pool_scale,
      w_br_conv, w_br_attn, w_br_pool)


def _outproj_kernel(m_ref, h_ref, mod_ref, w_ref, o_ref):
    mix = jnp.dot(m_ref[...], w_ref[0], preferred_element_type=F32)
    o_ref[...] = h_ref[...] + mod_ref[0, 0][2:3, :] * mix


def _outproj(merged, h, ada, w_out, layer, seq, tm=512):
    t, d = h.shape
    tm = min(tm, seq)
    tiles_per_seq = seq // tm
    return pl.pallas_call(
        _outproj_kernel,
        grid=(t // tm,),
        in_specs=[
            pl.BlockSpec((tm, d), lambda i: (i, 0)),
            pl.BlockSpec((tm, d), lambda i: (i, 0)),
            pl.BlockSpec((1, 1, 3, d), lambda i: (i // tiles_per_seq, 1, 0, 0)),
            pl.BlockSpec((1, d, d), lambda i: (layer, 0, 0)),
        ],
        out_specs=pl.BlockSpec((tm, d), lambda i: (i, 0)),
        out_shape=jax.ShapeDtypeStruct((t, d), F32),
        compiler_params=_cparams("arbitrary"),
        name="outproj",
    )(merged, h, ada, w_out)


def _prep_gate_up(w_gu, dff, dff_pad, tf):
    depth, d, _ = w_gu.shape
    pad = ((0, 0), (0, 0), (0, dff_pad - dff))
    gate = jnp.pad(w_gu[..., :dff], pad).reshape(depth, d, dff_pad // tf, 1, tf)
    up = jnp.pad(w_gu[..., dff:], pad).reshape(depth, d, dff_pad // tf, 1, tf)
    return jnp.concatenate([gate, up], axis=3).reshape(depth, d, 2 * dff_pad).astype(BF16)


def _prep_down(w_d, dff, dff_pad):
    return jnp.pad(w_d, ((0, 0), (0, dff_pad - dff), (0, 0))).astype(BF16)


def _cumsum_matrix(blk):
    j = lax.broadcasted_iota(jnp.int32, (2 * blk, blk), 0) % blk
    s = lax.broadcasted_iota(jnp.int32, (2 * blk, blk), 1)
    return jnp.where(j >= s, -1.0, 0.0).astype(BF16)


ATTN_BLOCK = 256
FFN_CHUNK = 512
ATTN_SKIP_BELOW = None


def kernel(x, c, norm_g, w_ada, b_ada, ffn1_w_gu, ffn1_w_d, w_in, conv_w, conv_b, conv_ln_g, conv_ln_b, pool_w,
           pool_scale, w_br_conv, w_br_attn, w_br_pool, w_out, ffn2_w_gu, ffn2_w_d, final_g):
    batch, seq, d = x.shape
    depth = norm_g.shape[0]
    t = batch * seq
    dff = ffn1_w_d.shape[1]
    dff_pad = -(-dff // FFN_CHUNK) * FFN_CHUNK
    d_conv2 = 2 * conv_w.shape[2]
    d_attn = w_br_attn.shape[1]
    d_pool = pool_scale.shape[1]

    wgu1 = _prep_gate_up(ffn1_w_gu, dff, dff_pad, FFN_CHUNK)
    wgu2 = _prep_gate_up(ffn2_w_gu, dff, dff_pad, FFN_CHUNK)
    wd1 = _prep_down(ffn1_w_d, dff, dff_pad)
    wd2 = _prep_down(ffn2_w_d, dff, dff_pad)
    w_in_b = w_in.astype(BF16)
    w_out_b = w_out.astype(BF16)
    wbc = w_br_conv.astype(BF16)
    wba = w_br_attn.astype(BF16)
    wbp = w_br_pool.astype(BF16)
    pool_w_b = pool_w.astype(BF16)
    conv_vec = jnp.stack([conv_b, conv_ln_g, conv_ln_b], axis=1)
    pool_scale3 = pool_scale.reshape(depth, 1, d_pool)
    norm_g3 = norm_g.reshape(depth * 3, 1, d)
    final_g2 = final_g.reshape(1, d)
    cum_mat = _cumsum_matrix(ATTN_BLOCK)

    rows = 8
    c_pad = jnp.zeros((rows, d), F32).at[:batch].set(c)
    ada = _ada_all_layers(c_pad, w_ada, b_ada)[:, :batch]
    ada = ada.reshape(depth, batch, N_ADA_CHUNKS // 3, 3, d)

    h = x.reshape(t, d)
    for layer in range(depth):
        ada_l = ada[layer]
        h = _ffn(h, ada_l, norm_g3, wgu1, wd1, layer, 0, seq, FFN_CHUNK)
        conv_in, qkv, pool_in, gates = _proj(h, ada_l, norm_g3, w_in_b, layer, seq, d_conv2, 3 * d_attn, d_pool)
        attn = _attention(qkv, cum_mat, batch, seq, d_attn, ATTN_BLOCK, ATTN_SKIP_BELOW)
        merged = _branches(conv_in, pool_in, attn, gates, conv_w, conv_vec, pool_w_b, pool_scale3, wbc, wba, wbp,
                           layer, seq, d)
        h = _outproj(merged, h, ada_l, w_out_b, layer, seq)
        last = final_g2 if layer == depth - 1 else None
        h = _ffn(h, ada_l, norm_g3, wgu2, wd2, layer, 2, seq, FFN_CHUNK, final_g=last)
    return h.reshape(batch, seq, d)
```

```python
import functools

import jax
import jax.numpy as jnp
from jax import lax
from jax.experimental import pallas as pl
from jax.experimental.pallas import tpu as pltpu

F32 = jnp.float32
BF16 = jnp.bfloat16

NORM_EPS = 1e-6
HEAD_DIM = 64
POOL_WINDOWS = (2, 4, 8, 16)
N_ADA_CHUNKS = 9
LANES = 128
V7X_VMEM_LIMIT = 60000 * 1024
CONV_HALO = 32
POOL_HALO = 16


def _sigmoid(x):
    return 1.0 / (1.0 + jnp.exp(-x))


def _cparams(*sem):
    return pltpu.CompilerParams(dimension_semantics=sem, vmem_limit_bytes=V7X_VMEM_LIMIT)


def _modulate(h, g, mod):
    ms = jnp.mean(h * h, axis=-1, keepdims=True)
    y = h * lax.rsqrt(ms + NORM_EPS) * g
    return y * (1.0 + mod[1:2, :]) + mod[0:1, :]


def _ada_kernel(c_ref, w_ref, b_ref, o_ref):
    c = c_ref[...]
    act = (c * _sigmoid(c)).astype(BF16)
    o_ref[0] = jnp.dot(act, w_ref[0].astype(BF16), preferred_element_type=F32) + b_ref[0]


def _ada_all_layers(c_pad, w_ada, b_ada, tn=1024):
    depth, d, n = w_ada.shape
    rows = c_pad.shape[0]
    return pl.pallas_call(
        _ada_kernel,
        grid=(depth, n // tn),
        in_specs=[
            pl.BlockSpec((rows, d), lambda l, j: (0, 0)),
            pl.BlockSpec((1, d, tn), lambda l, j: (l, 0, j)),
            pl.BlockSpec((1, 1, tn), lambda l, j: (l, 0, j)),
        ],
        out_specs=pl.BlockSpec((1, rows, tn), lambda l, j: (l, 0, j)),
        out_shape=jax.ShapeDtypeStruct((depth, rows, n), F32),
        compiler_params=_cparams("arbitrary", "arbitrary"),
        name="ada",
    )(c_pad, w_ada, b_ada.reshape(depth, 1, n))


def _ffn_kernel(h_ref, mod_ref, g_ref, wgu_ref, wd_ref, *rest, tf, nk, final):
    if final:
        fg_ref, o_ref, u_scr = rest
    else:
        o_ref, u_scr = rest
    k = pl.program_id(1)

    @pl.when(k == 0)
    def _():
        u_scr[...] = _modulate(h_ref[...], g_ref[0], mod_ref[0, 0]).astype(BF16)
        o_ref[...] = jnp.zeros_like(o_ref)

    gu = jnp.dot(u_scr[...], wgu_ref[0], preferred_element_type=F32)
    gate = gu[:, :tf]
    act = (gate * _sigmoid(gate) * gu[:, tf:]).astype(BF16)
    o_ref[...] += jnp.dot(act, wd_ref[0], preferred_element_type=F32)

    @pl.when(k == nk - 1)
    def _():
        gate_row = mod_ref[0, 0][2:3, :]
        h_new = h_ref[...] + (0.5 * gate_row) * o_ref[...]
        if final:
            ms = jnp.mean(h_new * h_new, axis=-1, keepdims=True)
            h_new = h_new * lax.rsqrt(ms + NORM_EPS) * fg_ref[...]
        o_ref[...] = h_new


def _ffn(h, ada, norm_g, wgu, wd, layer, sub, seq, tf, final_g=None, tm=512):
    t, d = h.shape
    tm = min(tm, seq)
    nk = wd.shape[1] // tf
    tiles_per_seq = seq // tm
    final = final_g is not None
    in_specs = [
        pl.BlockSpec((tm, d), lambda i, k: (i, 0)),
        pl.BlockSpec((1, 1, 3, d), lambda i, k: (i // tiles_per_seq, sub, 0, 0)),
        pl.BlockSpec((1, 1, d), lambda i, k: (layer * 3 + sub, 0, 0)),
        pl.BlockSpec((1, d, 2 * tf), lambda i, k: (layer, 0, k)),
        pl.BlockSpec((1, tf, d), lambda i, k: (layer, k, 0)),
    ]
    args = [h, ada, norm_g, wgu, wd]
    if final:
        in_specs.append(pl.BlockSpec((1, d), lambda i, k: (0, 0)))
        args.append(final_g)
    return pl.pallas_call(
        functools.partial(_ffn_kernel, tf=tf, nk=nk, final=final),
        grid=(t // tm, nk),
        in_specs=in_specs,
        out_specs=pl.BlockSpec((tm, d), lambda i, k: (i, 0)),
        out_shape=jax.ShapeDtypeStruct((t, d), F32),
        scratch_shapes=[pltpu.VMEM((tm, d), BF16)],
        compiler_params=_cparams("arbitrary", "arbitrary"),
        name="ffn",
    )(*args)


def _proj_kernel(h_ref, mod_ref, g_ref, w_ref, conv_ref, qkv_ref, pool_ref, gate_ref, u_scr, *, bounds, q_tiles):
    j = pl.program_id(1)
    conv_end, qkv_end, pool_end = bounds

    @pl.when(j == 0)
    def _():
        u_scr[...] = _modulate(h_ref[...], g_ref[0], mod_ref[0, 0]).astype(BF16)

    def tile():
        return jnp.dot(u_scr[...], w_ref[0], preferred_element_type=F32)

    @pl.when(j < conv_end)
    def _():
        conv_ref[...] = tile()

    @pl.when(jnp.logical_and(j >= conv_end, j < conv_end + q_tiles))
    def _():
        qkv_ref[...] = (tile() * (HEAD_DIM ** -0.5)).astype(BF16)

    @pl.when(jnp.logical_and(j >= conv_end + q_tiles, j < qkv_end))
    def _():
        qkv_ref[...] = tile().astype(BF16)

    @pl.when(jnp.logical_and(j >= qkv_end, j < pool_end))
    def _():
        pool_ref[...] = tile()

    @pl.when(j >= pool_end)
    def _():
        gate_ref[...] = _sigmoid(tile()).astype(BF16)


def _proj(h, ada, norm_g, w_in, layer, seq, d_conv2, d_attn3, d_pool, tm=1024, tn=512):
    t, d = h.shape
    n = w_in.shape[2]
    d_gate = n - d_conv2 - d_attn3 - d_pool
    conv_end = d_conv2 // tn
    qkv_end = conv_end + d_attn3 // tn
    pool_end = qkv_end + d_pool // tn
    q_tiles = d_attn3 // 3 // tn
    tm = min(tm, seq)
    tiles_per_seq = seq // tm

    def clamp(j, lo, count):
        return jnp.clip(j - lo, 0, count - 1)

    return pl.pallas_call(
        functools.partial(_proj_kernel, bounds=(conv_end, qkv_end, pool_end), q_tiles=q_tiles),
        grid=(t // tm, n // tn),
        in_specs=[
            pl.BlockSpec((tm, d), lambda i, j: (i, 0)),
            pl.BlockSpec((1, 1, 3, d), lambda i, j: (i // tiles_per_seq, 1, 0, 0)),
            pl.BlockSpec((1, 1, d), lambda i, j: (layer * 3 + 1, 0, 0)),
            pl.BlockSpec((1, d, tn), lambda i, j: (layer, 0, j)),
        ],
        out_specs=[
            pl.BlockSpec((tm, tn), lambda i, j: (i, clamp(j, 0, conv_end))),
            pl.BlockSpec((tm, tn), lambda i, j: (i, clamp(j, conv_end, qkv_end - conv_end))),
            pl.BlockSpec((tm, tn), lambda i, j: (i, clamp(j, qkv_end, pool_end - qkv_end))),
            pl.BlockSpec((tm, tn), lambda i, j: (i, clamp(j, pool_end, d_gate // tn))),
        ],
        out_shape=[
            jax.ShapeDtypeStruct((t, d_conv2), F32),
            jax.ShapeDtypeStruct((t, d_attn3), BF16),
            jax.ShapeDtypeStruct((t, d_pool), F32),
            jax.ShapeDtypeStruct((t, d_gate), BF16),
        ],
        scratch_shapes=[pltpu.VMEM((tm, d), BF16)],
        compiler_params=_cparams("arbitrary", "arbitrary"),
        name="proj",
    )(h, ada, norm_g, w_in)


def _attn_kernel(q_ref, k_ref, v_ref, u_ref, o_ref, acc_ref, c_ref, *, blk, skip_below):
    qi = pl.program_id(2)
    lane = lax.broadcasted_iota(jnp.int32, (blk, LANES), 1)
    lower_head = lane < HEAD_DIM
    q = q_ref[...]
    zero = jnp.zeros_like(q)
    q2 = jnp.concatenate([jnp.where(lower_head, q, zero), jnp.where(lower_head, zero, q)], axis=0)
    row = lax.broadcasted_iota(jnp.int32, (2 * blk, blk), 0)
    col = lax.broadcasted_iota(jnp.int32, (2 * blk, blk), 1)
    causal = col < jnp.where(row >= blk, row - blk, row)

    def tile(kb, diagonal):
        start = pl.multiple_of(kb * blk, blk)
        k = k_ref[pl.ds(start, blk), :]
        v = v_ref[pl.ds(start, blk), :]
        z = lax.dot_general(q2, k, (((1,), (1,)), ((), ())), preferred_element_type=F32)
        sp = jnp.maximum(z, 0.0) + jnp.log(1.0 + jnp.exp(-jnp.abs(z)))
        if diagonal:
            sp = jnp.where(causal, sp, 0.0)
        hi = pltpu.bitcast(pltpu.bitcast(sp, jnp.uint32) & jnp.uint32(0xFFFF0000), F32)
        lo = sp - hi
        parts = jnp.concatenate([hi.astype(BF16), lo.astype(BF16)], axis=1)
        cum = jnp.dot(parts, u_ref[...], preferred_element_type=F32)
        w = jnp.exp(z + cum)
        if diagonal:
            w = jnp.where(causal, w, 0.0)
        pv = jnp.dot(w.astype(BF16), v, preferred_element_type=F32)
        return pv, jnp.broadcast_to(cum[:, 0:1], (2 * blk, LANES))

    pv, total = tile(qi, True)
    acc_ref[...] = pv
    c_ref[...] = total

    def step(kb):
        pv, total = tile(kb, False)
        c = c_ref[...]
        acc_ref[...] += pv * jnp.exp(c)
        c_ref[...] = c + total

    if skip_below is None:
        def body(n, carry):
            step(qi - 1 - n)
            return carry
        lax.fori_loop(0, qi, body, 0)
    else:
        def cond(carry):
            kb, live = carry
            return jnp.logical_and(kb >= 0, live)

        def body(carry):
            kb, _ = carry
            step(kb)
            return kb - 1, jnp.max(c_ref[...]) > skip_below
        lax.while_loop(cond, body, (qi - 1, jnp.max(c_ref[...]) > skip_below))

    o_ref[...] = jnp.where(lower_head, acc_ref[0:blk, :], acc_ref[blk:2 * blk, :]).astype(o_ref.dtype)


def _attention(qkv, cum_mat, batch, seq, d_attn, blk, skip_below):
    t = qkv.shape[0]
    nq = seq // blk
    pairs = d_attn // LANES
    return pl.pallas_call(
        functools.partial(_attn_kernel, blk=blk, skip_below=skip_below),
        grid=(batch, pairs, nq),
        in_specs=[
            pl.BlockSpec((blk, LANES), lambda b, p, i: (b * nq + i, p)),
            pl.BlockSpec((seq, LANES), lambda b, p, i: (b, pairs + p)),
            pl.BlockSpec((seq, LANES), lambda b, p, i: (b, 2 * pairs + p)),
            pl.BlockSpec((2 * blk, blk), lambda b, p, i: (0, 0)),
        ],
        out_specs=pl.BlockSpec((blk, LANES), lambda b, p, i: (b * nq + i, p)),
        out_shape=jax.ShapeDtypeStruct((t, d_attn), BF16),
        scratch_shapes=[pltpu.VMEM((2 * blk, LANES), F32), pltpu.VMEM((2 * blk, LANES), F32)],
        compiler_params=_cparams("arbitrary", "arbitrary", "arbitrary"),
        name="attn",
    )(qkv, qkv, qkv, cum_mat)


def _branch_kernel(ci_ref, ch_ref, pi_ref, ph_ref, at_ref, gt_ref, cw_ref, cv_ref, pw_ref, ps_ref,
                   wbc_ref, wba_ref, wbp_ref, o_ref, glu_scr, sw_scr, pin_scr, pd_scr, *, tm, tiles_per_seq, chunk):
    i = pl.program_id(0)
    tile_in_seq = i % tiles_per_seq
    first = tile_in_seq == 0
    dc = ci_ref.shape[1] // 2
    d = o_ref.shape[1]
    width = cw_ref.shape[1]

    halo = ch_ref[:, :dc] * _sigmoid(ch_ref[:, dc:])
    glu_scr[0:CONV_HALO, :] = jnp.where(first, 0.0, halo)
    glu_scr[CONV_HALO:, :] = ci_ref[:, :dc] * _sigmoid(ci_ref[:, dc:])
    cvec = cv_ref[0]
    for r0 in range(0, tm, chunk):
        acc = jnp.zeros((chunk, dc), F32)
        for tap in range(width):
            off = CONV_HALO - (width - 1) + tap + r0
            acc = acc + glu_scr[off:off + chunk, :] * cw_ref[0, tap:tap + 1, :]
        dw = acc + cvec[0:1, :]
        mu = jnp.mean(dw, axis=-1, keepdims=True)
        cen = dw - mu
        var = jnp.mean(cen * cen, axis=-1, keepdims=True)
        y = cen * lax.rsqrt(var + NORM_EPS) * cvec[1:2, :] + cvec[2:3, :]
        sw_scr[r0:r0 + chunk, :] = (y * _sigmoid(y)).astype(BF16)
    y_conv = jnp.dot(sw_scr[...], wbc_ref[0], preferred_element_type=F32)
    merged = gt_ref[:, 0:d].astype(F32) * y_conv

    y_attn = jnp.dot(at_ref[...], wba_ref[0], preferred_element_type=F32)
    merged = merged + gt_ref[:, d:2 * d].astype(F32) * y_attn

    pin_scr[0:POOL_HALO, :] = jnp.where(first, 0.0, ph_ref[...])
    pin_scr[POOL_HALO:, :] = pi_ref[...]
    pos = tile_in_seq * tm + lax.broadcasted_iota(jnp.int32, (tm, 1), 0)
    group = pw_ref.shape[2]
    for gi, window in enumerate(POOL_WINDOWS):
        lo = gi * group
        total = pin_scr[POOL_HALO:POOL_HALO + tm, lo:lo + group]
        cur = total
        for back in range(1, window):
            total = total + pin_scr[POOL_HALO - back:POOL_HALO - back + tm, lo:lo + group]
        count = jnp.minimum(pos + 1, window).astype(F32)
        diff = total / count - cur
        mapped = jnp.dot(diff.astype(BF16), pw_ref[0, gi], preferred_element_type=F32)
        pd_scr[:, lo:lo + group] = (mapped * ps_ref[0, :, lo:lo + group]).astype(BF16)
    y_pool = jnp.dot(pd_scr[...], wbp_ref[0], preferred_element_type=F32)
    merged = merged + gt_ref[:, 2 * d:3 * d].astype(F32) * y_pool

    o_ref[...] = merged.astype(o_ref.dtype)


def _branches(conv_in, pool_in, attn, gates, conv_w, conv_vec, pool_w, pool_scale, w_br_conv, w_br_attn, w_br_pool,
              layer, seq, d_model, tm=256, chunk=32):
    t, dc2 = conv_in.shape
    dc = dc2 // 2
    dp = pool_in.shape[1]
    da = attn.shape[1]
    width = conv_w.shape[1]
    assert width - 1 <= CONV_HALO and max(POOL_WINDOWS) - 1 <= POOL_HALO
    assert tm % CONV_HALO == 0 and tm % chunk == 0 and seq % tm == 0
    tiles_per_seq = seq // tm
    ngroups = len(POOL_WINDOWS)
    group = dp // ngroups

    def prev_rows(halo):
        return lambda i: (jnp.maximum(i * (tm // halo) - 1, 0), 0)

    const3 = lambda i: (layer, 0, 0)
    return pl.pallas_call(
        functools.partial(_branch_kernel, tm=tm, tiles_per_seq=tiles_per_seq, chunk=chunk),
        grid=(t // tm,),
        in_specs=[
            pl.BlockSpec((tm, dc2), lambda i: (i, 0)),
            pl.BlockSpec((CONV_HALO, dc2), prev_rows(CONV_HALO)),
            pl.BlockSpec((tm, dp), lambda i: (i, 0)),
            pl.BlockSpec((POOL_HALO, dp), prev_rows(POOL_HALO)),
            pl.BlockSpec((tm, da), lambda i: (i, 0)),
            pl.BlockSpec((tm, 3 * d_model), lambda i: (i, 0)),
            pl.BlockSpec((1, width, dc), const3),
            pl.BlockSpec((1, 3, dc), const3),
            pl.BlockSpec((1, ngroups, group, group), lambda i: (layer, 0, 0, 0)),
            pl.BlockSpec((1, 1, dp), const3),
            pl.BlockSpec((1, dc, d_model), const3),
            pl.BlockSpec((1, da, d_model), const3),
            pl.BlockSpec((1, dp, d_model), const3),
        ],
        out_specs=pl.BlockSpec((tm, d_model), lambda i: (i, 0)),
        out_shape=jax.ShapeDtypeStruct((t, d_model), BF16),
        scratch_shapes=[
            pltpu.VMEM((CONV_HALO + tm, dc), F32),
            pltpu.VMEM((tm, dc), BF16),
            pltpu.VMEM((POOL_HALO + tm, dp), F32),
            pltpu.VMEM((tm, dp), BF16),
        ],
        compiler_params=_cparams("arbitrary"),
        name="branches",
    )(conv_in, conv_in, pool_in, pool_in, attn, gates, conv_w, conv_vec, pool_w, pool_scale,
      w_br_conv, w_br_attn, w_br_pool)


def _outproj_kernel(m_ref, h_ref, mod_ref, w_ref, o_ref):
    mix = jnp.dot(m_ref[...], w_ref[0], preferred_element_type=F32)
    o_ref[...] = h_ref[...] + mod_ref[0, 0][2:3, :] * mix


def _outproj(merged, h, ada, w_out, layer, seq, tm=512):
    t, d = h.shape
    tm = min(tm, seq)
    tiles_per_seq = seq // tm
    return pl.pallas_call(
        _outproj_kernel,
        grid=(t // tm,),
        in_specs=[
            pl.BlockSpec((tm, d), lambda i: (i, 0)),
            pl.BlockSpec((tm, d), lambda i: (i, 0)),
            pl.BlockSpec((1, 1, 3, d), lambda i: (i // tiles_per_seq, 1, 0, 0)),
            pl.BlockSpec((1, d, d), lambda i: (layer, 0, 0)),
        ],
        out_specs=pl.BlockSpec((tm, d), lambda i: (i, 0)),
        out_shape=jax.ShapeDtypeStruct((t, d), F32),
        compiler_params=_cparams("arbitrary"),
        name="outproj",
    )(merged, h, ada, w_out)


def _prep_gate_up(w_gu, dff, dff_pad, tf):
    depth, d, _ = w_gu.shape
    pieces = []
    for lo in range(0, dff_pad, tf):
        hi = min(lo + tf, dff)
        for base in (0, dff):
            pieces.append(w_gu[..., base + lo:base + hi].astype(BF16))
            if hi - lo < tf:
                pieces.append(jnp.zeros((depth, d, tf - (hi - lo)), BF16))
    return jnp.concatenate(pieces, axis=-1)


def _prep_down(w_d, dff, dff_pad):
    depth, _, d = w_d.shape
    return jnp.concatenate([w_d.astype(BF16), jnp.zeros((depth, dff_pad - dff, d), BF16)], axis=1)


def _cumsum_matrix(blk):
    j = lax.broadcasted_iota(jnp.int32, (2 * blk, blk), 0) % blk
    s = lax.broadcasted_iota(jnp.int32, (2 * blk, blk), 1)
    return jnp.where(j >= s, -1.0, 0.0).astype(BF16)


ATTN_BLOCK = 256
FFN_CHUNK = 512
ATTN_SKIP_BELOW = -110.0


def kernel(x, c, norm_g, w_ada, b_ada, ffn1_w_gu, ffn1_w_d, w_in, conv_w, conv_b, conv_ln_g, conv_ln_b, pool_w,
           pool_scale, w_br_conv, w_br_attn, w_br_pool, w_out, ffn2_w_gu, ffn2_w_d, final_g):
    batch, seq, d = x.shape
    depth = norm_g.shape[0]
    t = batch * seq
    dff = ffn1_w_d.shape[1]
    dff_pad = -(-dff // FFN_CHUNK) * FFN_CHUNK
    d_conv2 = 2 * conv_w.shape[2]
    d_attn = w_br_attn.shape[1]
    d_pool = pool_scale.shape[1]

    wgu1 = _prep_gate_up(ffn1_w_gu, dff, dff_pad, FFN_CHUNK)
    wgu2 = _prep_gate_up(ffn2_w_gu, dff, dff_pad, FFN_CHUNK)
    wd1 = _prep_down(ffn1_w_d, dff, dff_pad)
    wd2 = _prep_down(ffn2_w_d, dff, dff_pad)
    w_in_b = w_in.astype(BF16)
    w_out_b = w_out.astype(BF16)
    wbc = w_br_conv.astype(BF16)
    wba = w_br_attn.astype(BF16)
    wbp = w_br_pool.astype(BF16)
    pool_w_b = pool_w.astype(BF16)
    conv_vec = jnp.stack([conv_b, conv_ln_g, conv_ln_b], axis=1)
    pool_scale3 = pool_scale.reshape(depth, 1, d_pool)
    norm_g3 = norm_g.reshape(depth * 3, 1, d)
    final_g2 = final_g.reshape(1, d)
    cum_mat = _cumsum_matrix(ATTN_BLOCK)

    rows = 8
    c_pad = jnp.zeros((rows, d), F32).at[:batch].set(c)
    ada = _ada_all_layers(c_pad, w_ada, b_ada)[:, :batch]
    ada = ada.reshape(depth, batch, N_ADA_CHUNKS // 3, 3, d)

    h = x.reshape(t, d)
    for layer in range(depth):
        ada_l = ada[layer]
        h = _ffn(h, ada_l, norm_g3, wgu1, wd1, layer, 0, seq, FFN_CHUNK)
        conv_in, qkv, pool_in, gates = _proj(h, ada_l, norm_g3, w_in_b, layer, seq, d_conv2, 3 * d_attn, d_pool)
        attn = _attention(qkv, cum_mat, batch, seq, d_attn, ATTN_BLOCK, ATTN_SKIP_BELOW)
        merged = _branches(conv_in, pool_in, attn, gates, conv_w, conv_vec, pool_w_b, pool_scale3, wbc, wba, wbp,
                           layer, seq, d)
        h = _outproj(merged, h, ada_l, w_out_b, layer, seq)
        last = final_g2 if layer == depth - 1 else None
        h = _ffn(h, ada_l, norm_g3, wgu2, wd2, layer, 2, seq, FFN_CHUNK, final_g=last)
    return h.reshape(batch, seq, d)
```

```python
import functools

import jax
import jax.numpy as jnp
from jax import lax
from jax.experimental import pallas as pl
from jax.experimental.pallas import tpu as pltpu

F32 = jnp.float32
BF16 = jnp.bfloat16

NORM_EPS = 1e-6
HEAD_DIM = 64
POOL_WINDOWS = (2, 4, 8, 16)
N_ADA_CHUNKS = 9
LANES = 128
V7X_VMEM_LIMIT = 60000 * 1024
CONV_HALO = 32
POOL_HALO = 16


def _sigmoid(x):
    return 1.0 / (1.0 + jnp.exp(-x))


def _cparams(*sem):
    return pltpu.CompilerParams(dimension_semantics=sem, vmem_limit_bytes=V7X_VMEM_LIMIT)


MOD_ROWS = 16
MOD_UNROLL = 8


def _modulate_into(u_scr, h_ref, g_ref, mod_ref, row_scr):
    d = h_ref.shape[1]
    mod = mod_ref[0, 0]
    row_scr[0:MOD_ROWS, :] = jnp.broadcast_to(g_ref[0] * (1.0 + mod[1:2, :]), (MOD_ROWS, d))
    row_scr[MOD_ROWS:, :] = jnp.broadcast_to(mod[0:1, :], (MOD_ROWS, d))

    def body(r, carry):
        rows = pl.ds(pl.multiple_of(r * MOD_ROWS, MOD_ROWS), MOD_ROWS)
        h = h_ref[rows, :]
        ms = jnp.mean(h * h, axis=-1, keepdims=True)
        y = h * lax.rsqrt(ms + NORM_EPS)
        u_scr[rows, :] = (y * row_scr[0:MOD_ROWS, :] + row_scr[MOD_ROWS:, :]).astype(u_scr.dtype)
        return carry
    lax.fori_loop(0, h_ref.shape[0] // MOD_ROWS, body, 0, unroll=MOD_UNROLL)


def _ada_kernel(c_ref, w_ref, b_ref, o_ref):
    c = c_ref[...]
    act = (c * _sigmoid(c)).astype(BF16)
    o_ref[0] = jnp.dot(act, w_ref[0].astype(BF16), preferred_element_type=F32) + b_ref[0]


def _ada_all_layers(c_pad, w_ada, b_ada, tn=1024):
    depth, d, n = w_ada.shape
    rows = c_pad.shape[0]
    return pl.pallas_call(
        _ada_kernel,
        grid=(depth, n // tn),
        in_specs=[
            pl.BlockSpec((rows, d), lambda l, j: (0, 0)),
            pl.BlockSpec((1, d, tn), lambda l, j: (l, 0, j)),
            pl.BlockSpec((1, 1, tn), lambda l, j: (l, 0, j)),
        ],
        out_specs=pl.BlockSpec((1, rows, tn), lambda l, j: (l, 0, j)),
        out_shape=jax.ShapeDtypeStruct((depth, rows, n), F32),
        compiler_params=_cparams("arbitrary", "arbitrary"),
        name="ada",
    )(c_pad, w_ada, b_ada.reshape(depth, 1, n))


def _ffn_kernel(h_ref, mod_ref, g_ref, wgu_ref, wd_ref, *rest, tf, nk, final):
    if final:
        fg_ref, o_ref, u_scr, row_scr = rest
    else:
        o_ref, u_scr, row_scr = rest
    k = pl.program_id(1)

    def chunk():
        gu = jnp.dot(u_scr[...], wgu_ref[0, 0], preferred_element_type=F32)
        gate = gu[:, :tf]
        act = (gate * _sigmoid(gate) * gu[:, tf:]).astype(BF16)
        return jnp.dot(act, wd_ref[0], preferred_element_type=F32)

    @pl.when(k == 0)
    def _():
        _modulate_into(u_scr, h_ref, g_ref, mod_ref, row_scr)
        o_ref[...] = chunk()

    @pl.when(k > 0)
    def _():
        o_ref[...] += chunk()

    @pl.when(k == nk - 1)
    def _():
        gate_row = mod_ref[0, 0][2:3, :]
        h_new = h_ref[...] + (0.5 * gate_row) * o_ref[...]
        if final:
            ms = jnp.mean(h_new * h_new, axis=-1, keepdims=True)
            h_new = h_new * lax.rsqrt(ms + NORM_EPS) * fg_ref[...]
        o_ref[...] = h_new


def _ffn(h, ada, norm_g, wgu, wd, layer, sub, seq, tf, final_g=None, tm=1024):
    t, d = h.shape
    tm = min(tm, seq)
    nk = wd.shape[1] // tf
    tiles_per_seq = seq // tm
    final = final_g is not None
    in_specs = [
        pl.BlockSpec((tm, d), lambda i, k: (i, 0)),
        pl.BlockSpec((1, 1, 3, d), lambda i, k: (i // tiles_per_seq, sub, 0, 0)),
        pl.BlockSpec((1, 1, d), lambda i, k: (layer * 3 + sub, 0, 0)),
        pl.BlockSpec((1, 1, d, 2 * tf), lambda i, k: (layer, k, 0, 0)),
        pl.BlockSpec((1, tf, d), lambda i, k: (layer, k, 0)),
    ]
    args = [h, ada, norm_g, wgu, wd]
    if final:
        in_specs.append(pl.BlockSpec((1, d), lambda i, k: (0, 0)))
        args.append(final_g)
    return pl.pallas_call(
        functools.partial(_ffn_kernel, tf=tf, nk=nk, final=final),
        grid=(t // tm, nk),
        in_specs=in_specs,
        out_specs=pl.BlockSpec((tm, d), lambda i, k: (i, 0)),
        out_shape=jax.ShapeDtypeStruct((t, d), F32),
        scratch_shapes=[pltpu.VMEM((tm, d), BF16), pltpu.VMEM((2 * MOD_ROWS, d), F32)],
        compiler_params=_cparams("arbitrary", "arbitrary"),
        name="ffn",
    )(*args)


def _proj_kernel(h_ref, mod_ref, g_ref, w_ref, conv_ref, qkv_ref, pool_ref, gate_ref, u_scr, row_scr, *, bounds,
                 q_tiles):
    j = pl.program_id(1)
    conv_end, qkv_end, pool_end = bounds

    @pl.when(j == 0)
    def _():
        _modulate_into(u_scr, h_ref, g_ref, mod_ref, row_scr)

    def tile():
        return jnp.dot(u_scr[...], w_ref[0, 0], preferred_element_type=F32)

    @pl.when(j < conv_end)
    def _():
        conv_ref[...] = tile()

    @pl.when(jnp.logical_and(j >= conv_end, j < conv_end + q_tiles))
    def _():
        qkv_ref[...] = (tile() * (HEAD_DIM ** -0.5)).astype(BF16)

    @pl.when(jnp.logical_and(j >= conv_end + q_tiles, j < qkv_end))
    def _():
        qkv_ref[...] = tile().astype(BF16)

    @pl.when(jnp.logical_and(j >= qkv_end, j < pool_end))
    def _():
        pool_ref[...] = tile()

    @pl.when(j >= pool_end)
    def _():
        gate_ref[...] = _sigmoid(tile()).astype(BF16)


def _proj(h, ada, norm_g, w_in, layer, seq, d_conv2, d_attn3, d_pool, tm=1024):
    t, d = h.shape
    tn = w_in.shape[3]
    n = w_in.shape[1] * tn
    d_gate = n - d_conv2 - d_attn3 - d_pool
    conv_end = d_conv2 // tn
    qkv_end = conv_end + d_attn3 // tn
    pool_end = qkv_end + d_pool // tn
    q_tiles = d_attn3 // 3 // tn
    tm = min(tm, seq)
    tiles_per_seq = seq // tm

    def clamp(j, lo, count):
        return jnp.clip(j - lo, 0, count - 1)

    return pl.pallas_call(
        functools.partial(_proj_kernel, bounds=(conv_end, qkv_end, pool_end), q_tiles=q_tiles),
        grid=(t // tm, n // tn),
        in_specs=[
            pl.BlockSpec((tm, d), lambda i, j: (i, 0)),
            pl.BlockSpec((1, 1, 3, d), lambda i, j: (i // tiles_per_seq, 1, 0, 0)),
            pl.BlockSpec((1, 1, d), lambda i, j: (layer * 3 + 1, 0, 0)),
            pl.BlockSpec((1, 1, d, tn), lambda i, j: (layer, j, 0, 0)),
        ],
        out_specs=[
            pl.BlockSpec((tm, tn), lambda i, j: (i, clamp(j, 0, conv_end))),
            pl.BlockSpec((tm, tn), lambda i, j: (i, clamp(j, conv_end, qkv_end - conv_end))),
            pl.BlockSpec((tm, tn), lambda i, j: (i, clamp(j, qkv_end, pool_end - qkv_end))),
            pl.BlockSpec((tm, tn), lambda i, j: (i, clamp(j, pool_end, d_gate // tn))),
        ],
        out_shape=[
            jax.ShapeDtypeStruct((t, d_conv2), F32),
            jax.ShapeDtypeStruct((t, d_attn3), BF16),
            jax.ShapeDtypeStruct((t, d_pool), F32),
            jax.ShapeDtypeStruct((t, d_gate), BF16),
        ],
        scratch_shapes=[pltpu.VMEM((tm, d), BF16), pltpu.VMEM((2 * MOD_ROWS, d), F32)],
        compiler_params=_cparams("arbitrary", "arbitrary"),
        name="proj",
    )(h, ada, norm_g, w_in)


def _attn_kernel(q_ref, k_ref, v_ref, u_ref, o_ref, acc_ref, c_ref, *, blk, skip_below):
    qi = pl.program_id(2)
    lane = lax.broadcasted_iota(jnp.int32, (blk, LANES), 1)
    lower_head = lane < HEAD_DIM
    q = q_ref[...]
    zero = jnp.zeros_like(q)
    q2 = jnp.concatenate([jnp.where(lower_head, q, zero), jnp.where(lower_head, zero, q)], axis=0)
    row = lax.broadcasted_iota(jnp.int32, (2 * blk, blk), 0)
    col = lax.broadcasted_iota(jnp.int32, (2 * blk, blk), 1)
    causal = col < jnp.where(row >= blk, row - blk, row)

    def tile(kb, diagonal):
        start = pl.multiple_of(kb * blk, blk)
        k = k_ref[pl.ds(start, blk), :]
        v = v_ref[pl.ds(start, blk), :]
        z = lax.dot_general(q2, k, (((1,), (1,)), ((), ())), preferred_element_type=F32)
        sp = jnp.maximum(z, 0.0) + jnp.log(1.0 + jnp.exp(-jnp.abs(z)))
        if diagonal:
            sp = jnp.where(causal, sp, 0.0)
        hi = pltpu.bitcast(pltpu.bitcast(sp, jnp.uint32) & jnp.uint32(0xFFFF0000), F32)
        lo = sp - hi
        parts = jnp.concatenate([hi.astype(BF16), lo.astype(BF16)], axis=1)
        cum = jnp.dot(parts, u_ref[...], preferred_element_type=F32)
        w = jnp.exp(z + cum)
        if diagonal:
            w = jnp.where(causal, w, 0.0)
        pv = jnp.dot(w.astype(BF16), v, preferred_element_type=F32)
        return pv, jnp.broadcast_to(cum[:, 0:1], (2 * blk, LANES))

    @pl.when(qi == 0)
    def _():
        pv, total = tile(qi, True)
        acc_ref[...] = pv
        c_ref[...] = total

    @pl.when(qi > 0)
    def _():
        pv_d, total_d = tile(qi, True)
        pv_n, total_n = tile(qi - 1, False)
        acc_ref[...] = pv_d + pv_n * jnp.exp(total_d)
        c_ref[...] = total_d + total_n

        def cond(carry):
            kb, live = carry
            return jnp.logical_and(kb >= 0, live)

        def body(carry):
            kb, _ = carry
            pv, total = tile(kb, False)
            c = c_ref[...]
            acc_ref[...] += pv * jnp.exp(c)
            c_ref[...] = c + total
            return kb - 1, jnp.max(c_ref[...]) > skip_below
        lax.while_loop(cond, body, (qi - 2, jnp.max(c_ref[...]) > skip_below))

    o_ref[...] = jnp.where(lower_head, acc_ref[0:blk, :], acc_ref[blk:2 * blk, :]).astype(o_ref.dtype)


def _attention(qkv, cum_mat, batch, seq, d_attn, blk, skip_below):
    t = qkv.shape[0]
    nq = seq // blk
    pairs = d_attn // LANES
    return pl.pallas_call(
        functools.partial(_attn_kernel, blk=blk, skip_below=skip_below),
        grid=(batch, pairs, nq),
        in_specs=[
            pl.BlockSpec((blk, LANES), lambda b, p, i: (b * nq + i, p)),
            pl.BlockSpec((seq, LANES), lambda b, p, i: (b, pairs + p)),
            pl.BlockSpec((seq, LANES), lambda b, p, i: (b, 2 * pairs + p)),
            pl.BlockSpec((2 * blk, blk), lambda b, p, i: (0, 0)),
        ],
        out_specs=pl.BlockSpec((blk, LANES), lambda b, p, i: (b * nq + i, p)),
        out_shape=jax.ShapeDtypeStruct((t, d_attn), BF16),
        scratch_shapes=[pltpu.VMEM((2 * blk, LANES), F32), pltpu.VMEM((2 * blk, LANES), F32)],
        compiler_params=_cparams("arbitrary", "arbitrary", "arbitrary"),
        name="attn",
    )(qkv, qkv, qkv, cum_mat)


def _branch_kernel(ci_ref, ch_ref, pi_ref, ph_ref, at_ref, gt_ref, cw_ref, cv_ref, pw_ref, ps_ref,
                   wbc_ref, wba_ref, wbp_ref, o_ref, glu_scr, sw_scr, pin_scr, pd_scr, *, tm, tiles_per_seq, chunk):
    i = pl.program_id(0)
    tile_in_seq = i % tiles_per_seq
    first = tile_in_seq == 0
    dc = ci_ref.shape[1] // 2
    d = o_ref.shape[1]
    width = cw_ref.shape[1]

    halo = ch_ref[:, :dc] * _sigmoid(ch_ref[:, dc:])
    glu_scr[0:CONV_HALO, :] = jnp.where(first, 0.0, halo)
    glu_scr[CONV_HALO:, :] = ci_ref[:, :dc] * _sigmoid(ci_ref[:, dc:])
    cvec = cv_ref[0]
    sub = 8
    for r0 in range(0, tm, chunk):
        base = CONV_HALO + r0 - sub
        acc = None
        for b in range(sub):
            part = None
            for a in range(-(-width // sub)):
                s = sub * a + b
                if s >= width:
                    continue
                lo = base - sub * a
                term = glu_scr[lo:lo + chunk + sub, :] * cw_ref[0, width - 1 - s:width - s, :]
                part = term if part is None else part + term
            if b:
                part = pltpu.roll(part, b, axis=0)
            acc = part if acc is None else acc + part
        dw = acc[sub:, :] + cvec[0:1, :]
        mu = jnp.mean(dw, axis=-1, keepdims=True)
        cen = dw - mu
        var = jnp.mean(cen * cen, axis=-1, keepdims=True)
        y = cen * lax.rsqrt(var + NORM_EPS) * cvec[1:2, :] + cvec[2:3, :]
        sw_scr[r0:r0 + chunk, :] = (y * _sigmoid(y)).astype(BF16)
    y_conv = jnp.dot(sw_scr[...], wbc_ref[0], preferred_element_type=F32)
    merged = gt_ref[:, 0:d].astype(F32) * y_conv

    y_attn = jnp.dot(at_ref[...], wba_ref[0], preferred_element_type=F32)
    merged = merged + gt_ref[:, d:2 * d].astype(F32) * y_attn

    pin_scr[0:POOL_HALO, :] = jnp.where(first, 0.0, ph_ref[...])
    pin_scr[POOL_HALO:, :] = pi_ref[...]
    pos = tile_in_seq * tm + lax.broadcasted_iota(jnp.int32, (tm, 1), 0)
    group = pw_ref.shape[2]
    for gi, window in enumerate(POOL_WINDOWS):
        lo = gi * group
        total = pin_scr[POOL_HALO:POOL_HALO + tm, lo:lo + group]
        cur = total
        for back in range(1, window):
            total = total + pin_scr[POOL_HALO - back:POOL_HALO - back + tm, lo:lo + group]
        count = jnp.minimum(pos + 1, window).astype(F32)
        diff = total / count - cur
        mapped = jnp.dot(diff.astype(BF16), pw_ref[0, gi], preferred_element_type=F32)
        pd_scr[:, lo:lo + group] = (mapped * ps_ref[0, :, lo:lo + group]).astype(BF16)
    y_pool = jnp.dot(pd_scr[...], wbp_ref[0], preferred_element_type=F32)
    merged = merged + gt_ref[:, 2 * d:3 * d].astype(F32) * y_pool

    o_ref[...] = merged.astype(o_ref.dtype)


def _branches(conv_in, pool_in, attn, gates, conv_w, conv_vec, pool_w, pool_scale, w_br_conv, w_br_attn, w_br_pool,
              layer, seq, d_model, tm=256, chunk=32):
    t, dc2 = conv_in.shape
    dc = dc2 // 2
    dp = pool_in.shape[1]
    da = attn.shape[1]
    width = conv_w.shape[1]
    assert -(-width // 8) * 8 <= CONV_HALO and max(POOL_WINDOWS) - 1 <= POOL_HALO
    assert tm % CONV_HALO == 0 and tm % chunk == 0 and seq % tm == 0
    tiles_per_seq = seq // tm
    ngroups = len(POOL_WINDOWS)
    group = dp // ngroups

    def prev_rows(halo):
        return lambda i: (jnp.maximum(i * (tm // halo) - 1, 0), 0)

    const3 = lambda i: (layer, 0, 0)
    return pl.pallas_call(
        functools.partial(_branch_kernel, tm=tm, tiles_per_seq=tiles_per_seq, chunk=chunk),
        grid=(t // tm,),
        in_specs=[
            pl.BlockSpec((tm, dc2), lambda i: (i, 0)),
            pl.BlockSpec((CONV_HALO, dc2), prev_rows(CONV_HALO)),
            pl.BlockSpec((tm, dp), lambda i: (i, 0)),
            pl.BlockSpec((POOL_HALO, dp), prev_rows(POOL_HALO)),
            pl.BlockSpec((tm, da), lambda i: (i, 0)),
            pl.BlockSpec((tm, 3 * d_model), lambda i: (i, 0)),
            pl.BlockSpec((1, width, dc), const3),
            pl.BlockSpec((1, 3, dc), const3),
            pl.BlockSpec((1, ngroups, group, group), lambda i: (layer, 0, 0, 0)),
            pl.BlockSpec((1, 1, dp), const3),
            pl.BlockSpec((1, dc, d_model), const3),
            pl.BlockSpec((1, da, d_model), const3),
            pl.BlockSpec((1, dp, d_model), const3),
        ],
        out_specs=pl.BlockSpec((tm, d_model), lambda i: (i, 0)),
        out_shape=jax.ShapeDtypeStruct((t, d_model), BF16),
        scratch_shapes=[
            pltpu.VMEM((CONV_HALO + tm, dc), F32),
            pltpu.VMEM((tm, dc), BF16),
            pltpu.VMEM((POOL_HALO + tm, dp), F32),
            pltpu.VMEM((tm, dp), BF16),
        ],
        compiler_params=_cparams("arbitrary"),
        name="branches",
    )(conv_in, conv_in, pool_in, pool_in, attn, gates, conv_w, conv_vec, pool_w, pool_scale,
      w_br_conv, w_br_attn, w_br_pool)


def _outproj_kernel(m_ref, h_ref, mod_ref, w_ref, o_ref):
    mix = jnp.dot(m_ref[...], w_ref[0], preferred_element_type=F32)
    o_ref[...] = h_ref[...] + mod_ref[0, 0][2:3, :] * mix


def _outproj(merged, h, ada, w_out, layer, seq, tm=512):
    t, d = h.shape
    tm = min(tm, seq)
    tiles_per_seq = seq // tm
    return pl.pallas_call(
        _outproj_kernel,
        grid=(t // tm,),
        in_specs=[
            pl.BlockSpec((tm, d), lambda i: (i, 0)),
            pl.BlockSpec((tm, d), lambda i: (i, 0)),
            pl.BlockSpec((1, 1, 3, d), lambda i: (i // tiles_per_seq, 1, 0, 0)),
            pl.BlockSpec((1, d, d), lambda i: (layer, 0, 0)),
        ],
        out_specs=pl.BlockSpec((tm, d), lambda i: (i, 0)),
        out_shape=jax.ShapeDtypeStruct((t, d), F32),
        compiler_params=_cparams("arbitrary"),
        name="outproj",
    )(merged, h, ada, w_out)


def _prep_gate_up(w_gu, dff, dff_pad, tf):
    depth, d, _ = w_gu.shape
    tiles = []
    for lo in range(0, dff_pad, tf):
        hi = min(lo + tf, dff)
        pieces = []
        for base in (0, dff):
            pieces.append(w_gu[..., base + lo:base + hi].astype(BF16))
            if hi - lo < tf:
                pieces.append(jnp.zeros((depth, d, tf - (hi - lo)), BF16))
        tiles.append(jnp.concatenate(pieces, axis=-1))
    return jnp.stack(tiles, axis=1)


def _prep_down(w_d, dff, dff_pad):
    depth, _, d = w_d.shape
    return jnp.concatenate([w_d.astype(BF16), jnp.zeros((depth, dff_pad - dff, d), BF16)], axis=1)


def _cumsum_matrix(blk):
    j = lax.broadcasted_iota(jnp.int32, (2 * blk, blk), 0) % blk
    s = lax.broadcasted_iota(jnp.int32, (2 * blk, blk), 1)
    return jnp.where(j >= s, -1.0, 0.0).astype(BF16)


ATTN_BLOCK = 256
FFN_CHUNK = 512
PROJ_TILE = 512
ATTN_SKIP_BELOW = -110.0


def kernel(x, c, norm_g, w_ada, b_ada, ffn1_w_gu, ffn1_w_d, w_in, conv_w, conv_b, conv_ln_g, conv_ln_b, pool_w,
           pool_scale, w_br_conv, w_br_attn, w_br_pool, w_out, ffn2_w_gu, ffn2_w_d, final_g):
    batch, seq, d = x.shape
    depth = norm_g.shape[0]
    t = batch * seq
    dff = ffn1_w_d.shape[1]
    dff_pad = -(-dff // FFN_CHUNK) * FFN_CHUNK
    d_conv2 = 2 * conv_w.shape[2]
    d_attn = w_br_attn.shape[1]
    d_pool = pool_scale.shape[1]

    wgu1 = _prep_gate_up(ffn1_w_gu, dff, dff_pad, FFN_CHUNK)
    wgu2 = _prep_gate_up(ffn2_w_gu, dff, dff_pad, FFN_CHUNK)
    wd1 = _prep_down(ffn1_w_d, dff, dff_pad)
    wd2 = _prep_down(ffn2_w_d, dff, dff_pad)
    w_in_b = w_in.astype(BF16).reshape(depth, d, -1, PROJ_TILE).transpose(0, 2, 1, 3)
    w_out_b = w_out.astype(BF16)
    wbc = w_br_conv.astype(BF16)
    wba = w_br_attn.astype(BF16)
    wbp = w_br_pool.astype(BF16)
    pool_w_b = pool_w.astype(BF16)
    conv_vec = jnp.stack([conv_b, conv_ln_g, conv_ln_b], axis=1)
    pool_scale3 = pool_scale.reshape(depth, 1, d_pool)
    norm_g3 = norm_g.reshape(depth * 3, 1, d)
    final_g2 = final_g.reshape(1, d)
    cum_mat = _cumsum_matrix(ATTN_BLOCK)

    rows = 8
    c_pad = jnp.zeros((rows, d), F32).at[:batch].set(c)
    ada = _ada_all_layers(c_pad, w_ada, b_ada)[:, :batch]
    ada = ada.reshape(depth, batch, N_ADA_CHUNKS // 3, 3, d)

    h = x.reshape(t, d)
    for layer in range(depth):
        ada_l = ada[layer]
        h = _ffn(h, ada_l, norm_g3, wgu1, wd1, layer, 0, seq, FFN_CHUNK)
        conv_in, qkv, pool_in, gates = _proj(h, ada_l, norm_g3, w_in_b, layer, seq, d_conv2, 3 * d_attn, d_pool)
        attn = _attention(qkv, cum_mat, batch, seq, d_attn, ATTN_BLOCK, ATTN_SKIP_BELOW)
        merged = _branches(conv_in, pool_in, attn, gates, conv_w, conv_vec, pool_w_b, pool_scale3, wbc, wba, wbp,
                           layer, seq, d)
        h = _outproj(merged, h, ada_l, w_out_b, layer, seq)
        last = final_g2 if layer == depth - 1 else None
        h = _ffn(h, ada_l, norm_g3, wgu2, wd2, layer, 2, seq, FFN_CHUNK, final_g=last)
    return h.reshape(batch, seq, d)
```

```python
import functools

import jax
import jax.numpy as jnp
from jax import lax
from jax.experimental import pallas as pl
from jax.experimental.pallas import tpu as pltpu

F32 = jnp.float32
BF16 = jnp.bfloat16

NORM_EPS = 1e-6
HEAD_DIM = 64
POOL_WINDOWS = (2, 4, 8, 16)
N_ADA_CHUNKS = 9
LANES = 128
V7X_VMEM_LIMIT = 60000 * 1024
CONV_HALO = 32
POOL_HALO = 16


def _sigmoid(x):
    return 1.0 / (1.0 + jnp.exp(-x))


def _cparams(*sem):
    return pltpu.CompilerParams(dimension_semantics=sem, vmem_limit_bytes=V7X_VMEM_LIMIT)


MOD_ROWS = 16
MOD_UNROLL = 8


def _modulate_into(u_scr, h_ref, g_ref, mod_ref, row_scr):
    d = h_ref.shape[1]
    mod = mod_ref[0, 0]
    row_scr[0:MOD_ROWS, :] = jnp.broadcast_to(g_ref[0] * (1.0 + mod[1:2, :]), (MOD_ROWS, d))
    row_scr[MOD_ROWS:, :] = jnp.broadcast_to(mod[0:1, :], (MOD_ROWS, d))

    def body(r, carry):
        rows = pl.ds(pl.multiple_of(r * MOD_ROWS, MOD_ROWS), MOD_ROWS)
        h = h_ref[rows, :]
        ms = jnp.mean(h * h, axis=-1, keepdims=True)
        y = h * lax.rsqrt(ms + NORM_EPS)
        u_scr[rows, :] = (y * row_scr[0:MOD_ROWS, :] + row_scr[MOD_ROWS:, :]).astype(u_scr.dtype)
        return carry
    lax.fori_loop(0, h_ref.shape[0] // MOD_ROWS, body, 0, unroll=MOD_UNROLL)


def _ada_kernel(c_ref, w_ref, b_ref, o_ref):
    c = c_ref[...]
    act = (c * _sigmoid(c)).astype(BF16)
    o_ref[0] = jnp.dot(act, w_ref[0].astype(BF16), preferred_element_type=F32) + b_ref[0]


def _ada_all_layers(c_pad, w_ada, b_ada, tn=1024):
    depth, d, n = w_ada.shape
    rows = c_pad.shape[0]
    return pl.pallas_call(
        _ada_kernel,
        grid=(depth, n // tn),
        in_specs=[
            pl.BlockSpec((rows, d), lambda l, j: (0, 0)),
            pl.BlockSpec((1, d, tn), lambda l, j: (l, 0, j)),
            pl.BlockSpec((1, 1, tn), lambda l, j: (l, 0, j)),
        ],
        out_specs=pl.BlockSpec((1, rows, tn), lambda l, j: (l, 0, j)),
        out_shape=jax.ShapeDtypeStruct((depth, rows, n), F32),
        compiler_params=_cparams("arbitrary", "arbitrary"),
        name="ada",
    )(c_pad, w_ada, b_ada.reshape(depth, 1, n))


def _ffn_kernel(h_ref, mod_ref, g_ref, wgu_ref, wd_ref, *rest, tf, nk, final):
    if final:
        fg_ref, o_ref, u_scr, row_scr = rest
    else:
        o_ref, u_scr, row_scr = rest
    k = pl.program_id(1)

    def chunk():
        gu = jnp.dot(u_scr[...], wgu_ref[0, 0], preferred_element_type=F32)
        gate = gu[:, :tf]
        act = (gate * _sigmoid(gate) * gu[:, tf:]).astype(BF16)
        return jnp.dot(act, wd_ref[0], preferred_element_type=F32)

    @pl.when(k == 0)
    def _():
        _modulate_into(u_scr, h_ref, g_ref, mod_ref, row_scr)
        o_ref[...] = chunk()

    @pl.when(k > 0)
    def _():
        o_ref[...] += chunk()

    @pl.when(k == nk - 1)
    def _():
        gate_row = mod_ref[0, 0][2:3, :]
        h_new = h_ref[...] + (0.5 * gate_row) * o_ref[...]
        if final:
            ms = jnp.mean(h_new * h_new, axis=-1, keepdims=True)
            h_new = h_new * lax.rsqrt(ms + NORM_EPS) * fg_ref[...]
        o_ref[...] = h_new


def _ffn(h, ada, norm_g, wgu, wd, layer, sub, seq, tf, final_g=None, tm=1024):
    t, d = h.shape
    tm = min(tm, seq)
    nk = wd.shape[1] // tf
    tiles_per_seq = seq // tm
    final = final_g is not None
    in_specs = [
        pl.BlockSpec((tm, d), lambda i, k: (i, 0)),
        pl.BlockSpec((1, 1, 3, d), lambda i, k: (i // tiles_per_seq, sub, 0, 0)),
        pl.BlockSpec((1, 1, d), lambda i, k: (layer * 3 + sub, 0, 0)),
        pl.BlockSpec((1, 1, d, 2 * tf), lambda i, k: (layer, k, 0, 0)),
        pl.BlockSpec((1, tf, d), lambda i, k: (layer, k, 0)),
    ]
    args = [h, ada, norm_g, wgu, wd]
    if final:
        in_specs.append(pl.BlockSpec((1, d), lambda i, k: (0, 0)))
        args.append(final_g)
    return pl.pallas_call(
        functools.partial(_ffn_kernel, tf=tf, nk=nk, final=final),
        grid=(t // tm, nk),
        in_specs=in_specs,
        out_specs=pl.BlockSpec((tm, d), lambda i, k: (i, 0)),
        out_shape=jax.ShapeDtypeStruct((t, d), F32),
        scratch_shapes=[pltpu.VMEM((tm, d), BF16), pltpu.VMEM((2 * MOD_ROWS, d), F32)],
        compiler_params=_cparams("arbitrary", "arbitrary"),
        name="ffn",
    )(*args)


def _proj_kernel(h_ref, mod_ref, g_ref, w_ref, conv_ref, qkv_ref, pool_ref, gate_ref, u_scr, row_scr, *, bounds,
                 q_tiles):
    j = pl.program_id(1)
    conv_end, qkv_end, pool_end = bounds

    @pl.when(j == 0)
    def _():
        _modulate_into(u_scr, h_ref, g_ref, mod_ref, row_scr)

    def tile():
        return jnp.dot(u_scr[...], w_ref[0], preferred_element_type=F32)

    @pl.when(j < conv_end)
    def _():
        conv_ref[...] = tile()

    @pl.when(jnp.logical_and(j >= conv_end, j < conv_end + q_tiles))
    def _():
        qkv_ref[...] = (tile() * (HEAD_DIM ** -0.5)).astype(BF16)

    @pl.when(jnp.logical_and(j >= conv_end + q_tiles, j < qkv_end))
    def _():
        qkv_ref[...] = tile().astype(BF16)

    @pl.when(jnp.logical_and(j >= qkv_end, j < pool_end))
    def _():
        pool_ref[...] = tile()

    @pl.when(j >= pool_end)
    def _():
        gate_ref[...] = (0.5 * jnp.tanh(0.5 * tile()) + 0.5).astype(BF16)


def _proj(h, ada, norm_g, w_in, layer, seq, d_conv2, d_attn3, d_pool, tm=1024, tn=512):
    t, d = h.shape
    n = w_in.shape[2]
    d_gate = n - d_conv2 - d_attn3 - d_pool
    conv_end = d_conv2 // tn
    qkv_end = conv_end + d_attn3 // tn
    pool_end = qkv_end + d_pool // tn
    q_tiles = d_attn3 // 3 // tn
    tm = min(tm, seq)
    tiles_per_seq = seq // tm

    def clamp(j, lo, count):
        return jnp.clip(j - lo, 0, count - 1)

    return pl.pallas_call(
        functools.partial(_proj_kernel, bounds=(conv_end, qkv_end, pool_end), q_tiles=q_tiles),
        grid=(t // tm, n // tn),
        in_specs=[
            pl.BlockSpec((tm, d), lambda i, j: (i, 0)),
            pl.BlockSpec((1, 1, 3, d), lambda i, j: (i // tiles_per_seq, 1, 0, 0)),
            pl.BlockSpec((1, 1, d), lambda i, j: (layer * 3 + 1, 0, 0)),
            pl.BlockSpec((1, d, tn), lambda i, j: (layer, 0, j)),
        ],
        out_specs=[
            pl.BlockSpec((tm, tn), lambda i, j: (i, clamp(j, 0, conv_end))),
            pl.BlockSpec((tm, tn), lambda i, j: (i, clamp(j, conv_end, qkv_end - conv_end))),
            pl.BlockSpec((tm, tn), lambda i, j: (i, clamp(j, qkv_end, pool_end - qkv_end))),
            pl.BlockSpec((tm, tn), lambda i, j: (i, clamp(j, pool_end, d_gate // tn))),
        ],
        out_shape=[
            jax.ShapeDtypeStruct((t, d_conv2), F32),
            jax.ShapeDtypeStruct((t, d_attn3), BF16),
            jax.ShapeDtypeStruct((t, d_pool), F32),
            jax.ShapeDtypeStruct((t, d_gate), BF16),
        ],
        scratch_shapes=[pltpu.VMEM((tm, d), BF16), pltpu.VMEM((2 * MOD_ROWS, d), F32)],
        compiler_params=_cparams("arbitrary", "arbitrary"),
        name="proj",
    )(h, ada, norm_g, w_in)


def _attn_kernel(q_ref, k_ref, v_ref, u_ref, o_ref, acc_ref, c_ref, *, blk, skip_below):
    i = pl.program_id(2)
    lane = lax.broadcasted_iota(jnp.int32, (blk, LANES), 1)
    lower_head = lane < HEAD_DIM
    rows = 2 * blk
    row = lax.broadcasted_iota(jnp.int32, (rows, blk), 0)
    col = lax.broadcasted_iota(jnp.int32, (rows, blk), 1)
    causal = col < jnp.where(row >= blk, row - blk, row)

    def stack_heads(q):
        zero = jnp.zeros_like(q)
        return jnp.concatenate([jnp.where(lower_head, q, zero), jnp.where(lower_head, zero, q)], axis=0)

    q_sub = (stack_heads(q_ref[0:blk, :]), stack_heads(q_ref[blk:rows, :]))

    def tile(q2, kb, diagonal):
        start = pl.multiple_of(kb * blk, blk)
        k = k_ref[pl.ds(start, blk), :]
        v = v_ref[pl.ds(start, blk), :]
        z = lax.dot_general(q2, k, (((1,), (1,)), ((), ())), preferred_element_type=F32)
        sp = jnp.maximum(z, 0.0) + jnp.log(1.0 + jnp.exp(-jnp.abs(z)))
        if diagonal:
            sp = jnp.where(causal, sp, 0.0)
        cum = jnp.dot(sp.astype(BF16), u_ref[...], preferred_element_type=F32)
        w = jnp.exp(z + cum)
        if diagonal:
            w = jnp.where(causal, w, 0.0)
        pv = jnp.dot(w.astype(BF16), v, preferred_element_type=F32)
        return pv, jnp.broadcast_to(cum[:, 0:1], (rows, LANES))

    def head_block(sub, first_kb):
        pv_d, total_d = tile(q_sub[sub], first_kb, True)
        pv_n, total_n = tile(q_sub[sub], first_kb - 1, False)
        acc_ref[sub] = pv_d + pv_n * jnp.exp(total_d)
        c_ref[sub] = total_d + total_n

    def walk_left(sub, first_kb):
        def cond(carry):
            kb, live = carry
            return jnp.logical_and(kb >= 0, live)

        def body(carry):
            kb, _ = carry
            pv, total = tile(q_sub[sub], kb, False)
            c = c_ref[sub]
            acc_ref[sub] += pv * jnp.exp(c)
            c_ref[sub] = c + total
            return kb - 1, jnp.max(c_ref[sub]) > skip_below
        lax.while_loop(cond, body, (first_kb, jnp.max(c_ref[sub]) > skip_below))

    @pl.when(i == 0)
    def _():
        pv, _ = tile(q_sub[0], 0, True)
        acc_ref[0] = pv
        head_block(1, 1)

    @pl.when(i > 0)
    def _():
        head_block(0, 2 * i)
        head_block(1, 2 * i + 1)
        walk_left(0, 2 * i - 2)
        walk_left(1, 2 * i - 1)

    for sub in range(2):
        o_ref[sub * blk:(sub + 1) * blk, :] = jnp.where(
            lower_head, acc_ref[sub, 0:blk, :], acc_ref[sub, blk:rows, :]).astype(o_ref.dtype)


def _attention(qkv, cum_mat, batch, seq, d_attn, blk, skip_below):
    t = qkv.shape[0]
    steps = seq // (2 * blk)
    pairs = d_attn // LANES
    return pl.pallas_call(
        functools.partial(_attn_kernel, blk=blk, skip_below=skip_below),
        grid=(batch, pairs, steps),
        in_specs=[
            pl.BlockSpec((2 * blk, LANES), lambda b, p, i: (b * steps + i, p)),
            pl.BlockSpec((seq, LANES), lambda b, p, i: (b, pairs + p)),
            pl.BlockSpec((seq, LANES), lambda b, p, i: (b, 2 * pairs + p)),
            pl.BlockSpec((blk, blk), lambda b, p, i: (0, 0)),
        ],
        out_specs=pl.BlockSpec((2 * blk, LANES), lambda b, p, i: (b * steps + i, p)),
        out_shape=jax.ShapeDtypeStruct((t, d_attn), BF16),
        scratch_shapes=[pltpu.VMEM((2, 2 * blk, LANES), F32), pltpu.VMEM((2, 2 * blk, LANES), F32)],
        compiler_params=_cparams("arbitrary", "arbitrary", "arbitrary"),
        name="attn",
    )(qkv, qkv, qkv, cum_mat)


def _branch_kernel(ci_ref, ch_ref, pi_ref, ph_ref, at_ref, gt_ref, h_ref, mod_ref, cw_ref, cv_ref, pw_ref, ps_ref,
                   wbc_ref, wba_ref, wbp_ref, wo_ref, o_ref, glu_scr, sw_scr, pin_scr, pd_scr, *, tm, tiles_per_seq,
                   chunk):
    i = pl.program_id(0)
    tile_in_seq = i % tiles_per_seq
    first = tile_in_seq == 0
    dc = ci_ref.shape[1] // 2
    d = o_ref.shape[1]
    width = cw_ref.shape[1]

    halo = ch_ref[:, :dc] * _sigmoid(ch_ref[:, dc:])
    glu_scr[0:CONV_HALO, :] = jnp.where(first, 0.0, halo)
    glu_scr[CONV_HALO:, :] = ci_ref[:, :dc] * _sigmoid(ci_ref[:, dc:])
    cvec = cv_ref[0]
    sub = 8
    for r0 in range(0, tm, chunk):
        base = CONV_HALO + r0 - sub
        acc = None
        for b in range(sub):
            part = None
            for a in range(-(-width // sub)):
                s = sub * a + b
                if s >= width:
                    continue
                lo = base - sub * a
                term = glu_scr[lo:lo + chunk + sub, :] * cw_ref[0, width - 1 - s:width - s, :]
                part = term if part is None else part + term
            if b:
                part = pltpu.roll(part, b, axis=0)
            acc = part if acc is None else acc + part
        dw = acc[sub:, :] + cvec[0:1, :]
        mu = jnp.mean(dw, axis=-1, keepdims=True)
        cen = dw - mu
        var = jnp.mean(cen * cen, axis=-1, keepdims=True)
        y = cen * lax.rsqrt(var + NORM_EPS) * cvec[1:2, :] + cvec[2:3, :]
        sw_scr[r0:r0 + chunk, :] = (y * _sigmoid(y)).astype(BF16)
    y_conv = jnp.dot(sw_scr[...], wbc_ref[0], preferred_element_type=F32)
    merged = gt_ref[:, 0:d].astype(F32) * y_conv

    y_attn = jnp.dot(at_ref[...], wba_ref[0], preferred_element_type=F32)
    merged = merged + gt_ref[:, d:2 * d].astype(F32) * y_attn

    pin_scr[0:POOL_HALO, :] = jnp.where(first, 0.0, ph_ref[...])
    pin_scr[POOL_HALO:, :] = pi_ref[...]
    pos = tile_in_seq * tm + lax.broadcasted_iota(jnp.int32, (tm, 1), 0)
    group = pw_ref.shape[2]
    for gi, window in enumerate(POOL_WINDOWS):
        lo = gi * group
        total = pin_scr[POOL_HALO:POOL_HALO + tm, lo:lo + group]
        cur = total
        for back in range(1, window):
            total = total + pin_scr[POOL_HALO - back:POOL_HALO - back + tm, lo:lo + group]
        count = jnp.minimum(pos + 1, window).astype(F32)
        diff = total / count - cur
        mapped = jnp.dot(diff.astype(BF16), pw_ref[0, gi], preferred_element_type=F32)
        pd_scr[:, lo:lo + group] = (mapped * ps_ref[0, :, lo:lo + group]).astype(BF16)
    y_pool = jnp.dot(pd_scr[...], wbp_ref[0], preferred_element_type=F32)
    merged = merged + gt_ref[:, 2 * d:3 * d].astype(F32) * y_pool

    mix = jnp.dot(merged.astype(BF16), wo_ref[0], preferred_element_type=F32)
    o_ref[...] = h_ref[...] + mod_ref[0, 0][2:3, :] * mix


def _mixer_tail(conv_in, pool_in, attn, gates, h, ada, conv_w, conv_vec, pool_w, pool_scale, w_br_conv, w_br_attn,
                w_br_pool, w_out, layer, seq, tm=256, chunk=32):
    t, dc2 = conv_in.shape
    d_model = h.shape[1]
    dc = dc2 // 2
    dp = pool_in.shape[1]
    da = attn.shape[1]
    width = conv_w.shape[1]
    assert -(-width // 8) * 8 <= CONV_HALO and max(POOL_WINDOWS) - 1 <= POOL_HALO
    assert tm % CONV_HALO == 0 and tm % chunk == 0 and seq % tm == 0
    tiles_per_seq = seq // tm
    ngroups = len(POOL_WINDOWS)
    group = dp // ngroups

    def prev_rows(halo):
        return lambda i: (jnp.maximum(i * (tm // halo) - 1, 0), 0)

    const3 = lambda i: (layer, 0, 0)

    def resident(shape):
        return pl.BlockSpec(shape, const3, pipeline_mode=pl.Buffered(1))

    return pl.pallas_call(
        functools.partial(_branch_kernel, tm=tm, tiles_per_seq=tiles_per_seq, chunk=chunk),
        grid=(t // tm,),
        in_specs=[
            pl.BlockSpec((tm, dc2), lambda i: (i, 0)),
            pl.BlockSpec((CONV_HALO, dc2), prev_rows(CONV_HALO)),
            pl.BlockSpec((tm, dp), lambda i: (i, 0)),
            pl.BlockSpec((POOL_HALO, dp), prev_rows(POOL_HALO)),
            pl.BlockSpec((tm, da), lambda i: (i, 0)),
            pl.BlockSpec((tm, 3 * d_model), lambda i: (i, 0)),
            pl.BlockSpec((tm, d_model), lambda i: (i, 0)),
            pl.BlockSpec((1, 1, 3, d_model), lambda i: (i // tiles_per_seq, 1, 0, 0)),
            pl.BlockSpec((1, width, dc), const3),
            pl.BlockSpec((1, 3, dc), const3),
            pl.BlockSpec((1, ngroups, group, group), lambda i: (layer, 0, 0, 0)),
            pl.BlockSpec((1, 1, dp), const3),
            resident((1, dc, d_model)),
            resident((1, da, d_model)),
            resident((1, dp, d_model)),
            resident((1, d_model, d_model)),
        ],
        out_specs=pl.BlockSpec((tm, d_model), lambda i: (i, 0)),
        out_shape=jax.ShapeDtypeStruct((t, d_model), F32),
        scratch_shapes=[
            pltpu.VMEM((CONV_HALO + tm, dc), F32),
            pltpu.VMEM((tm, dc), BF16),
            pltpu.VMEM((POOL_HALO + tm, dp), F32),
            pltpu.VMEM((tm, dp), BF16),
        ],
        compiler_params=_cparams("arbitrary"),
        name="mixer_tail",
    )(conv_in, conv_in, pool_in, pool_in, attn, gates, h, ada, conv_w, conv_vec, pool_w, pool_scale,
      w_br_conv, w_br_attn, w_br_pool, w_out)


def _prep_gate_up(w_gu, dff, dff_pad, tf):
    depth, d, _ = w_gu.shape
    tiles = []
    for lo in range(0, dff_pad, tf):
        hi = min(lo + tf, dff)
        pieces = []
        for base in (0, dff):
            pieces.append(w_gu[..., base + lo:base + hi].astype(BF16))
            if hi - lo < tf:
                pieces.append(jnp.zeros((depth, d, tf - (hi - lo)), BF16))
        tiles.append(jnp.concatenate(pieces, axis=-1))
    return jnp.stack(tiles, axis=1)


def _prep_down(w_d, dff, dff_pad):
    depth, _, d = w_d.shape
    return jnp.concatenate([w_d.astype(BF16), jnp.zeros((depth, dff_pad - dff, d), BF16)], axis=1)


def _cumsum_matrix(blk):
    j = lax.broadcasted_iota(jnp.int32, (blk, blk), 0)
    s = lax.broadcasted_iota(jnp.int32, (blk, blk), 1)
    return jnp.where(j >= s, -1.0, 0.0).astype(BF16)


ATTN_BLOCK = 256
FFN_CHUNK = 512
ATTN_SKIP_BELOW = -110.0


def kernel(x, c, norm_g, w_ada, b_ada, ffn1_w_gu, ffn1_w_d, w_in, conv_w, conv_b, conv_ln_g, conv_ln_b, pool_w,
           pool_scale, w_br_conv, w_br_attn, w_br_pool, w_out, ffn2_w_gu, ffn2_w_d, final_g):
    batch, seq, d = x.shape
    depth = norm_g.shape[0]
    t = batch * seq
    dff = ffn1_w_d.shape[1]
    dff_pad = -(-dff // FFN_CHUNK) * FFN_CHUNK
    d_conv2 = 2 * conv_w.shape[2]
    d_attn = w_br_attn.shape[1]
    d_pool = pool_scale.shape[1]

    wgu1 = _prep_gate_up(ffn1_w_gu, dff, dff_pad, FFN_CHUNK)
    wgu2 = _prep_gate_up(ffn2_w_gu, dff, dff_pad, FFN_CHUNK)
    wd1 = _prep_down(ffn1_w_d, dff, dff_pad)
    wd2 = _prep_down(ffn2_w_d, dff, dff_pad)
    w_in_b = w_in.astype(BF16)
    w_out_b = w_out.astype(BF16)
    wbc = w_br_conv.astype(BF16)
    wba = w_br_attn.astype(BF16)
    wbp = w_br_pool.astype(BF16)
    pool_w_b = pool_w.astype(BF16)
    conv_vec = jnp.stack([conv_b, conv_ln_g, conv_ln_b], axis=1)
    pool_scale3 = pool_scale.reshape(depth, 1, d_pool)
    norm_g3 = norm_g.reshape(depth * 3, 1, d)
    final_g2 = final_g.reshape(1, d)
    cum_mat = _cumsum_matrix(ATTN_BLOCK)

    rows = 8
    c_pad = jnp.zeros((rows, d), F32).at[:batch].set(c)
    ada = _ada_all_layers(c_pad, w_ada, b_ada)[:, :batch]
    ada = ada.reshape(depth, batch, N_ADA_CHUNKS // 3, 3, d)

    h = x.reshape(t, d)
    for layer in range(depth):
        ada_l = ada[layer]
        h = _ffn(h, ada_l, norm_g3, wgu1, wd1, layer, 0, seq, FFN_CHUNK)
        conv_in, qkv, pool_in, gates = _proj(h, ada_l, norm_g3, w_in_b, layer, seq, d_conv2, 3 * d_attn, d_pool)
        attn = _attention(qkv, cum_mat, batch, seq, d_attn, ATTN_BLOCK, ATTN_SKIP_BELOW)
        h = _mixer_tail(conv_in, pool_in, attn, gates, h, ada_l, conv_w, conv_vec, pool_w_b, pool_scale3, wbc, wba,
                        wbp, w_out_b, layer, seq)
        last = final_g2 if layer == depth - 1 else None
        h = _ffn(h, ada_l, norm_g3, wgu2, wd2, layer, 2, seq, FFN_CHUNK, final_g=last)
    return h.reshape(batch, seq, d)
```
